```python
import jax, jax.numpy as jnp
from jax import lax
import numpy as np

D_MODEL = 2048
BATCH = 16
SEQ = 2048
DEPTH = 4

N_MIXERS = 2
N_A_LAYERS = (DEPTH + 1) // 2
N_B_LAYERS = DEPTH // 2

HG_EXPAND = 128
HG_HEADS = D_MODEL // HG_EXPAND
HG_FDIM = HG_HEADS * HG_EXPAND
HG_VDIM = D_MODEL
HG_HEAD_V = HG_VDIM // HG_HEADS
HG_CHUNK = 64
HG_SUB = 16
LB_FLOOR = 1e-30

GM_WIDTH = D_MODEL
GM_GROUPS = 16
GM_GROUP_DIM = GM_WIDTH // GM_GROUPS
GM_CHUNK = 128

FFN_HIDDEN = -(-8 * D_MODEL // (3 * 256)) * 256

PLE_DIM = 256

EPS = 1e-6

kernel_name = "hgrn2_gmlp_interleaved_sandwich_ple"


def _rmsnorm(x, g):
    xf = x.astype(jnp.float32)
    y = xf * lax.rsqrt(jnp.mean(xf * xf, axis=-1, keepdims=True) + EPS)
    return (y * g.astype(jnp.float32)).astype(x.dtype)


def _layernorm(x, g, b):
    xf = x.astype(jnp.float32)
    mu = jnp.mean(xf, axis=-1, keepdims=True)
    xc = xf - mu
    y = xc * lax.rsqrt(jnp.mean(xc * xc, axis=-1, keepdims=True) + EPS)
    return (y * g.astype(jnp.float32) + b.astype(jnp.float32)).astype(x.dtype)


def _hgrn_lower_bounds(lb_logits):
    sm = jax.nn.softmax(lb_logits.astype(jnp.float32), axis=0)
    return jnp.cumsum(sm, axis=0) - sm[:1]


def _hgrn2_chunkwise(q, k, v, log_f):
    B, S, H, K = q.shape
    V = v.shape[-1]
    C, L = HG_CHUNK, HG_SUB
    n = C // L
    nc = S // C

    def to_chunks(t):
        return t.reshape(B, nc, C, H, t.shape[-1]).transpose(1, 0, 3, 2, 4)

    causal = jnp.tril(jnp.ones((L, L), dtype=bool))[:, :, None]
    sub_mask = jnp.tril(jnp.ones((n, n), dtype=bool), k=-1)[:, :, None, None]
    eye_n = jnp.eye(n, dtype=jnp.float32)[:, None, :, None]

    def step(state, inp):
        qc, kc, vc, gc = inp
        b = jnp.cumsum(gc, axis=2)
        o_inter = jnp.einsum('bhtk,bhkv->bhtv', qc * jnp.exp(b), state)
        qs = qc.reshape(B, H, n, L, K)
        ks = kc.reshape(B, H, n, L, K)
        bs = b.reshape(B, H, n, L, K)
        gs = gc.reshape(B, H, n, L, K)
        bstart = bs[:, :, :, :1, :] - gs[:, :, :, :1, :]
        qf = qs * jnp.exp(bs - bstart)
        expo = bstart[:, :, :, None] - bs[:, :, None]
        kf = jnp.where(sub_mask, ks[:, :, None] * jnp.exp(jnp.where(sub_mask, expo, 0.0)), 0.0)
        a_off = jnp.einsum('bhjtk,bhjisk->bhjtis', qf, kf)
        diff = bs[:, :, :, :, None, :] - bs[:, :, :, None, :, :]
        wdec = jnp.where(causal, jnp.exp(jnp.where(causal, diff, 0.0)), 0.0)
        a_diag = jnp.einsum('bhjtk,bhjsk,bhjtsk->bhjts', qs, ks, wdec)
        a = (a_off + a_diag[:, :, :, :, None, :] * eye_n).reshape(B, H, C, C)
        o_intra = jnp.einsum('bhts,bhsv->bhtv', a, vc)
        b_last = b[:, :, -1:, :]
        new_state = (jnp.exp(b_last[:, :, 0, :])[..., None] * state
                     + jnp.einsum('bhsk,bhsv->bhkv', kc * jnp.exp(b_last - b), vc))
        return new_state, o_inter + o_intra

    state0 = jnp.zeros((B, H, K, V), dtype=jnp.float32)
    _, o = lax.scan(step, state0, (to_chunks(q), to_chunks(k), to_chunks(v), to_chunks(log_f)))
    return o.transpose(1, 0, 3, 2, 4).reshape(B, S, H, V)


def _hgrn2_mixer(h, w_in, lb, out_norm, w_out):
    B, S, _ = h.shape
    proj = h @ w_in
    zq, zf, zi, zg = jnp.split(proj, [HG_FDIM, 2 * HG_FDIM, 2 * HG_FDIM + HG_VDIM], axis=-1)
    q = jax.nn.silu(zq.astype(jnp.float32))
    zf = zf.astype(jnp.float32)
    log_lb = jnp.log(jnp.maximum(lb, LB_FLOOR))
    log_f = jnp.logaddexp(log_lb, jnp.log1p(-lb) + jax.nn.log_sigmoid(zf))
    k = (1.0 - lb) * jax.nn.sigmoid(-zf)
    v = zi.astype(jnp.float32)
    heads = lambda t, d: t.reshape(B, S, HG_HEADS, d)
    o = _hgrn2_chunkwise(heads(q, HG_EXPAND), heads(k, HG_EXPAND),
                         heads(v, HG_HEAD_V), heads(log_f, HG_EXPAND))
    o = _rmsnorm(o, out_norm) * jax.nn.silu(heads(zg.astype(jnp.float32), HG_HEAD_V))
    return o.reshape(B, S, HG_VDIM).astype(h.dtype) @ w_out


def _chunk_gmlp_mixer(h, w_in, ln_g, ln_b, w_s, b_s, w_out):
    B, S, _ = h.shape
    z = jax.nn.gelu(h @ w_in, approximate=False)
    u, v = jnp.split(z, 2, axis=-1)
    v = _layernorm(v, ln_g, ln_b)
    v = v.reshape(B, S // GM_CHUNK, GM_CHUNK, GM_GROUPS, GM_GROUP_DIM)
    ws = w_s * jnp.tril(jnp.ones((GM_CHUNK, GM_CHUNK), dtype=w_s.dtype))
    sv = jnp.einsum('gts,bnsgc->bntgc', ws, v) + b_s.T[None, None, :, :, None]
    y = u * sv.reshape(B, S, GM_WIDTH)
    return y @ w_out


def _swiglu(h, w_gate, w_up, w_down):
    return (jax.nn.silu(h @ w_gate) * (h @ w_up)) @ w_down


def setup_inputs(seed: int = 0) -> dict:
    key = jax.random.key(seed)
    ks = jax.random.split(key, 24)
    f32 = jnp.float32
    nrm = lambda k, shape, s: (jax.random.normal(k, shape, f32) * s).astype(f32)
    gain = lambda k, shape: 1.0 + nrm(k, shape, 0.02)
    D = D_MODEL
    return {
        "x": nrm(ks[0], (BATCH, SEQ, D), 1.0),
        "p": nrm(ks[1], (DEPTH, BATCH, SEQ, PLE_DIM), 1.0),
        "hg_w_in": nrm(ks[2], (N_A_LAYERS, D, 2 * HG_FDIM + 2 * HG_VDIM), D ** -0.5),
        "hg_lb_logits": nrm(ks[3], (N_A_LAYERS, HG_FDIM), 0.5),
        "hg_out_norm": gain(ks[4], (N_A_LAYERS, HG_HEAD_V)),
        "hg_w_out": nrm(ks[5], (N_A_LAYERS, HG_VDIM, D), HG_VDIM ** -0.5),
        "gm_w_in": nrm(ks[6], (N_B_LAYERS, D, 2 * GM_WIDTH), D ** -0.5),
        "gm_ln_g": gain(ks[7], (N_B_LAYERS, GM_WIDTH)),
        "gm_ln_b": nrm(ks[8], (N_B_LAYERS, GM_WIDTH), 0.02),
        "gm_w_s": nrm(ks[9], (N_B_LAYERS, GM_GROUPS, GM_CHUNK, GM_CHUNK), GM_CHUNK ** -0.5),
        "gm_b_s": 1.0 + nrm(ks[10], (N_B_LAYERS, GM_GROUPS, GM_CHUNK), 0.1),
        "gm_w_out": nrm(ks[11], (N_B_LAYERS, GM_WIDTH, D), GM_WIDTH ** -0.5),
        "norm_mix_pre": gain(ks[12], (DEPTH, D)),
        "norm_mix_post": gain(ks[13], (DEPTH, D)),
        "norm_ffn_pre": gain(ks[14], (DEPTH, D)),
        "norm_ffn_post": gain(ks[15], (DEPTH, D)),
        "ffn_w_gate": nrm(ks[16], (DEPTH, D, FFN_HIDDEN), D ** -0.5),
        "ffn_w_up": nrm(ks[17], (DEPTH, D, FFN_HIDDEN), D ** -0.5),
        "ffn_w_down": nrm(ks[18], (DEPTH, FFN_HIDDEN, D), FFN_HIDDEN ** -0.5),
        "ple_w_proj": nrm(ks[19], (DEPTH, PLE_DIM, D), PLE_DIM ** -0.5),
        "ple_w_gate": nrm(ks[20], (DEPTH, D, D), D ** -0.5),
        "ple_norm": gain(ks[21], (DEPTH, D)),
    }


def reference(x, p, hg_w_in, hg_lb_logits, hg_out_norm, hg_w_out, gm_w_in, gm_ln_g, gm_ln_b,
              gm_w_s, gm_b_s, gm_w_out, norm_mix_pre, norm_mix_post, norm_ffn_pre, norm_ffn_post,
              ffn_w_gate, ffn_w_up, ffn_w_down, ple_w_proj, ple_w_gate, ple_norm):
    lbs = _hgrn_lower_bounds(hg_lb_logits)
    h = x
    for i in range(DEPTH):
        j = i // N_MIXERS
        a = _rmsnorm(h, norm_mix_pre[i])
        if i % N_MIXERS == 0:
            m = _hgrn2_mixer(a, hg_w_in[j], lbs[j], hg_out_norm[j], hg_w_out[j])
        else:
            m = _chunk_gmlp_mixer(a, gm_w_in[j], gm_ln_g[j], gm_ln_b[j], gm_w_s[j], gm_b_s[j],
                                  gm_w_out[j])
        h = h + _rmsnorm(m, norm_mix_post[i])
        f = _swiglu(_rmsnorm(h, norm_ffn_pre[i]), ffn_w_gate[i], ffn_w_up[i], ffn_w_down[i])
        h = h + _rmsnorm(f, norm_ffn_post[i])
        e = p[i].astype(h.dtype) @ ple_w_proj[i]
        gate = jax.nn.sigmoid(h @ ple_w_gate[i])
        h = h + _rmsnorm(e * gate, ple_norm[i])
    return h
```

```python
import functools

import jax
import jax.numpy as jnp
from jax import lax
from jax.experimental import pallas as pl
from jax.experimental.pallas import tpu as pltpu

F32 = jnp.float32
BF16 = jnp.bfloat16

EPS = 1e-6
LB_FLOOR = 1e-30
HG_EXPAND = 128
HG_CHUNK = 64
HG_SUB = 16
GM_CHUNK = 128
GM_GROUPS = 16

V7X_VMEM_BYTES = 64 * 1024 * 1024
VMEM_LIMIT_BYTES = V7X_VMEM_BYTES - 8 * 1024 * 1024
SAFE_DECAY_RANGE = 60.0


def _params(*semantics):
    return pltpu.CompilerParams(dimension_semantics=semantics, vmem_limit_bytes=VMEM_LIMIT_BYTES)


def _rms(x, gain):
    return x * lax.rsqrt(jnp.mean(x * x, axis=-1, keepdims=True) + EPS) * gain


def _sigmoid(x):
    return 1.0 / (1.0 + jnp.exp(-x))


def _silu(x):
    return x * _sigmoid(x)


def _row(v):
    return v.reshape(1, -1).astype(F32)


def _lower_bounds_kernel(logits_ref, lb_ref):
    x = logits_ref[...]
    n = x.shape[0]
    m = jnp.max(x, axis=0, keepdims=True)
    e = jnp.exp(x - m)
    sm = e / jnp.sum(e, axis=0, keepdims=True)
    run = sm[0:1] - sm[0:1]
    rows = []
    for r in range(n):
        run = run + sm[r:r + 1]
        rows.append(run - sm[0:1])
    lb_ref[...] = jnp.concatenate(rows, axis=0)


def _lower_bounds(logits):
    return pl.pallas_call(
        _lower_bounds_kernel,
        out_shape=jax.ShapeDtypeStruct(logits.shape, F32),
        name="hgrn_lower_bounds",
    )(logits.astype(F32))


def _prenorm_kernel(x_ref, g_ref, o_ref):
    o_ref[...] = _rms(x_ref[...], g_ref[...]).astype(o_ref.dtype)


def _prenorm(x, gain, tm=512):
    t, d = x.shape
    return pl.pallas_call(
        _prenorm_kernel,
        grid=(t // tm,),
        in_specs=[pl.BlockSpec((tm, d), lambda i: (i, 0)), pl.BlockSpec((1, d), lambda i: (0, 0))],
        out_specs=pl.BlockSpec((tm, d), lambda i: (i, 0)),
        out_shape=jax.ShapeDtypeStruct((t, d), BF16),
        compiler_params=_params("parallel"),
        name="prenorm",
    )(x, _row(gain))


def _mm_kernel(*refs, epilogue, n_extra, n_out):
    x_ref, w_ref = refs[0], refs[1]
    extras = refs[2:2 + n_extra]
    outs = refs[2 + n_extra:2 + n_extra + n_out]
    z = jnp.dot(x_ref[...], w_ref[...], preferred_element_type=F32)
    res = epilogue(z, *[e[...] for e in extras])
    for o_ref, r in zip(outs, res):
        o_ref[...] = r.astype(o_ref.dtype)


def _mm(x, w, col0, n_cols, epilogue, extras, out_dtypes, *, tm, tn, name):
    t, k = x.shape
    assert col0 % tn == 0 and n_cols % tn == 0 and t % tm == 0
    jb0 = col0 // tn
    in_specs = [pl.BlockSpec((tm, k), lambda i, j: (i, 0)),
                pl.BlockSpec((k, tn), lambda i, j: (0, jb0 + j))]
    in_specs += [pl.BlockSpec((1, tn), lambda i, j: (0, j)) for _ in extras]
    return pl.pallas_call(
        functools.partial(_mm_kernel, epilogue=epilogue, n_extra=len(extras), n_out=len(out_dtypes)),
        grid=(t // tm, n_cols // tn),
        in_specs=in_specs,
        out_specs=[pl.BlockSpec((tm, tn), lambda i, j: (i, j)) for _ in out_dtypes],
        out_shape=[jax.ShapeDtypeStruct((t, n_cols), dt) for dt in out_dtypes],
        compiler_params=_params("parallel", "arbitrary"),
        name=name,
    )(x, w, *extras)


def _ep_silu(z):
    return (_silu(z),)


def _ep_identity(z):
    return (z,)


def _ep_forget(z, lb):
    log_lb = jnp.log(jnp.maximum(lb, LB_FLOOR))
    log_sig = jnp.minimum(z, 0.0) - jnp.log1p(jnp.exp(-jnp.abs(z)))
    c = jnp.log1p(-lb) + log_sig
    log_f = jnp.maximum(log_lb, c) + jnp.log1p(jnp.exp(-jnp.abs(log_lb - c)))
    kk = (1.0 - lb) * _sigmoid(-z)
    return log_f, kk


def _gelu_exact(z):
    return 0.5 * z * (1.0 + lax.erf(z * (2.0 ** -0.5)))


def _ep_gelu(z):
    return (_gelu_exact(z),)


def _ep_gelu_layernorm(z, g, b):
    y = _gelu_exact(z)
    mu = jnp.mean(y, axis=-1, keepdims=True)
    yc = y - mu
    y = yc * lax.rsqrt(jnp.mean(yc * yc, axis=-1, keepdims=True) + EPS)
    return (y * g + b,)


def _hgrn_rec_kernel(q_ref, k_ref, g_ref, v_ref, sg_ref, gain_ref, y_ref, b_scr, st_scr, *, hb, seq):
    C, L, E = HG_CHUNK, HG_SUB, HG_EXPAND
    n_sub = C // L
    n_chunks = seq // C
    W = hb * E
    contract_last = (((1,), (1,)), ((), ()))
    contract_first = (((0,), (0,)), ((), ()))

    row = lax.broadcasted_iota(jnp.int32, (C, C), 0)
    col = lax.broadcasted_iota(jnp.int32, (C, C), 1)
    tril = (row >= col).astype(BF16)

    def cumsum_chunk(c, rmax):
        r0 = pl.multiple_of(c * C, C)
        g = g_ref[pl.ds(r0, C), :]
        g1 = g.astype(BF16)
        r1 = g - g1.astype(F32)
        g2 = r1.astype(BF16)
        g3 = (r1 - g2.astype(F32)).astype(BF16)
        b = (jnp.dot(tril, g1, preferred_element_type=F32)
             + jnp.dot(tril, g2, preferred_element_type=F32)
             + jnp.dot(tril, g3, preferred_element_type=F32))
        b_scr[pl.ds(r0, C), :] = b
        for j in range(n_sub):
            bstart = b[j * L:j * L + 1] - g[j * L:j * L + 1]
            rmax = jnp.maximum(rmax, bstart - b[(j + 1) * L - 1:(j + 1) * L])
        return rmax

    rmax = lax.fori_loop(0, n_chunks, cumsum_chunk, jnp.zeros((1, W), F32))
    safe = jnp.max(rmax) <= SAFE_DECAY_RANGE

    keeps = [lax.broadcasted_iota(jnp.int32, (L, (j + 1) * L), 1)
             <= lax.broadcasted_iota(jnp.int32, (L, (j + 1) * L), 0) + j * L for j in range(n_sub)]
    s_idx = lax.broadcasted_iota(jnp.int32, (L, 1), 0)

    def run(factorised_diag):
        st_scr[...] = jnp.zeros(st_scr.shape, F32)

        def chunk(c, carry):
            r0 = pl.multiple_of(c * C, C)
            for h in range(hb):
                cols = slice(h * E, (h + 1) * E)
                q = q_ref[pl.ds(r0, C), cols].astype(F32)
                k = k_ref[pl.ds(r0, C), cols].astype(F32)
                v = v_ref[pl.ds(r0, C), cols]
                g = g_ref[pl.ds(r0, C), cols]
                b = b_scr[pl.ds(r0, C), cols]
                st = st_scr[h]

                qb = (q * jnp.exp(b)).astype(BF16)
                o = lax.dot_general(qb, st.astype(BF16), contract_last, preferred_element_type=F32)

                pieces = []
                for j in range(n_sub):
                    lo, hi = j * L, (j + 1) * L
                    bstart = b[lo:lo + 1] - g[lo:lo + 1]
                    qf = (q[lo:hi] * jnp.exp(b[lo:hi] - bstart)).astype(BF16)
                    if factorised_diag:
                        kall = (k[:hi] * jnp.exp(bstart - b[:hi])).astype(BF16)
                        a = lax.dot_general(qf, kall, contract_last, preferred_element_type=F32)
                        keep = keeps[j]
                        a = jnp.where(keep, a, 0.0)
                        oj = jnp.dot(a.astype(BF16), v[:hi], preferred_element_type=F32)
                    else:
                        if j > 0:
                            kf = (k[:lo] * jnp.exp(bstart - b[:lo])).astype(BF16)
                            a = lax.dot_general(qf, kf, contract_last, preferred_element_type=F32)
                            oj = jnp.dot(a.astype(BF16), v[:lo], preferred_element_type=F32)
                        else:
                            oj = jnp.zeros((L, E), F32)
                        qj, kj, bj = q[lo:hi], k[lo:hi], b[lo:hi]
                        vj = v[lo:hi].astype(F32)
                        rows = []
                        for t in range(L):
                            m = s_idx <= t
                            wdec = jnp.where(m, jnp.exp(jnp.where(m, bj[t:t + 1] - bj, 0.0)), 0.0)
                            a_col = jnp.sum(wdec * qj[t:t + 1] * kj, axis=-1, keepdims=True)
                            rows.append(jnp.sum(a_col * vj, axis=0, keepdims=True))
                        oj = oj + jnp.concatenate(rows, axis=0)
                    pieces.append(oj)
                o = o + jnp.concatenate(pieces, axis=0)

                bl = b[C - 1:C]
                kdec = (k * jnp.exp(bl - b)).astype(BF16)
                upd = lax.dot_general(v, kdec, contract_first, preferred_element_type=F32)
                st_scr[h] = st * jnp.exp(bl) + upd

                y = _rms(o, gain_ref[...]) * sg_ref[pl.ds(r0, C), cols].astype(F32)
                y_ref[pl.ds(r0, C), cols] = y.astype(y_ref.dtype)
            return carry

        lax.fori_loop(0, n_chunks, chunk, 0)

    @pl.when(safe)
    def _():
        run(True)

    @pl.when(jnp.logical_not(safe))
    def _():
        run(False)


def _hgrn_recurrence(q, k, log_f, v, sg, gain, *, batch, seq, hb=2):
    t, d = q.shape
    w = hb * HG_EXPAND
    spec = pl.BlockSpec((seq, w), lambda b, h: (b, h))
    return pl.pallas_call(
        functools.partial(_hgrn_rec_kernel, hb=hb, seq=seq),
        grid=(batch, d // w),
        in_specs=[spec, spec, spec, spec, spec, pl.BlockSpec((1, HG_EXPAND), lambda b, h: (0, 0))],
        out_specs=spec,
        out_shape=jax.ShapeDtypeStruct((t, d), BF16),
        scratch_shapes=[pltpu.VMEM((seq, w), F32), pltpu.VMEM((hb, HG_EXPAND, HG_EXPAND), F32)],
        compiler_params=_params("parallel", "parallel"),
        name="hgrn_recurrence",
    )(q, k, log_f, v, sg, _row(gain))


def _residual_epilogue(m, h, gpost, gnext, hout_ref, anext_ref):
    hn = h + _rms(m, gpost)
    hout_ref[...] = hn
    anext_ref[...] = _rms(hn, gnext).astype(anext_ref.dtype)


def _mm_res_kernel(y_ref, w_ref, h_ref, gpost_ref, gnext_ref, hout_ref, anext_ref, *scratch, nk):
    if nk == 1:
        m = jnp.dot(y_ref[...], w_ref[...], preferred_element_type=F32)
        _residual_epilogue(m, h_ref[...], gpost_ref[...], gnext_ref[...], hout_ref, anext_ref)
        return
    acc_ref, = scratch
    kk = pl.program_id(1)

    @pl.when(kk == 0)
    def _():
        acc_ref[...] = jnp.zeros(acc_ref.shape, F32)

    acc_ref[...] += jnp.dot(y_ref[...], w_ref[...], preferred_element_type=F32)

    @pl.when(kk == nk - 1)
    def _():
        _residual_epilogue(acc_ref[...], h_ref[...], gpost_ref[...], gnext_ref[...], hout_ref, anext_ref)


def _mm_residual(y, w, h, gpost, gnext, *, tm, tk, name):
    t, k = y.shape
    d = w.shape[1]
    nk = k // tk
    assert k % tk == 0 and t % tm == 0
    row_spec = pl.BlockSpec((tm, d), lambda i, kk: (i, 0))
    vec_spec = pl.BlockSpec((1, d), lambda i, kk: (0, 0))
    return pl.pallas_call(
        functools.partial(_mm_res_kernel, nk=nk),
        grid=(t // tm, nk),
        in_specs=[pl.BlockSpec((tm, tk), lambda i, kk: (i, kk)),
                  pl.BlockSpec((tk, d), lambda i, kk: (kk, 0)),
                  row_spec, vec_spec, vec_spec],
        out_specs=[row_spec, row_spec],
        out_shape=[jax.ShapeDtypeStruct((t, d), F32), jax.ShapeDtypeStruct((t, d), BF16)],
        scratch_shapes=[] if nk == 1 else [pltpu.VMEM((tm, d), F32)],
        compiler_params=_params("parallel", "arbitrary"),
        name=name,
    )(y, w, h, _row(gpost), _row(gnext))


def _sgu_out_kernel(u_ref, v_ref, ws_ref, bst_ref, w_ref, h_ref, gpost_ref, gnext_ref,
                    hout_ref, anext_ref, y_scr, *, tm):
    P, G = GM_CHUNK, GM_GROUPS
    gd = u_ref.shape[1] // G
    r = lax.broadcasted_iota(jnp.int32, (P, P), 0)
    c = lax.broadcasted_iota(jnp.int32, (P, P), 1)
    causal = r >= c

    def chunk(ci, carry):
        r0 = pl.multiple_of(ci * P, P)
        for g in range(G):
            cols = slice(g * gd, (g + 1) * gd)
            wg = jnp.where(causal, ws_ref[g], jnp.zeros((), ws_ref.dtype))
            sv = jnp.dot(wg, v_ref[pl.ds(r0, P), cols], preferred_element_type=F32)
            sv = sv + bst_ref[:, g:g + 1]
            y = u_ref[pl.ds(r0, P), cols].astype(F32) * sv
            y_scr[pl.ds(r0, P), cols] = y.astype(y_scr.dtype)
        return carry

    lax.fori_loop(0, tm // P, chunk, 0)
    m = jnp.dot(y_scr[...], w_ref[...], preferred_element_type=F32)
    _residual_epilogue(m, h_ref[...], gpost_ref[...], gnext_ref[...], hout_ref, anext_ref)


def _sgu_out(u, v, ws, bs_t, w, h, gpost, gnext, *, tm):
    t, d = u.shape
    row_spec = pl.BlockSpec((tm, d), lambda i: (i, 0))
    vec_spec = pl.BlockSpec((1, d), lambda i: (0, 0))
    return pl.pallas_call(
        functools.partial(_sgu_out_kernel, tm=tm),
        grid=(t // tm,),
        in_specs=[row_spec, row_spec,
                  pl.BlockSpec(ws.shape, lambda i: (0, 0, 0)),
                  pl.BlockSpec(bs_t.shape, lambda i: (0, 0)),
                  pl.BlockSpec(w.shape, lambda i: (0, 0)),
                  row_spec, vec_spec, vec_spec],
        out_specs=[row_spec, row_spec],
        out_shape=[jax.ShapeDtypeStruct((t, d), F32), jax.ShapeDtypeStruct((t, d), BF16)],
        scratch_shapes=[pltpu.VMEM((tm, d), BF16)],
        compiler_params=_params("parallel"),
        name="gmlp_sgu_out",
    )(u, v, ws, bs_t, w, h, _row(gpost), _row(gnext))


def _ffn_in_kernel(x_ref, wg_ref, wu_ref, o_ref):
    x = x_ref[...]
    gate = jnp.dot(x, wg_ref[...], preferred_element_type=F32)
    up = jnp.dot(x, wu_ref[...], preferred_element_type=F32)
    o_ref[...] = (_silu(gate) * up).astype(o_ref.dtype)


def _ffn_in(x, wg, wu, *, tm, tn):
    t, k = x.shape
    n = wg.shape[1]
    assert n % tn == 0 and t % tm == 0
    w_spec = pl.BlockSpec((k, tn), lambda i, j: (0, j))
    return pl.pallas_call(
        _ffn_in_kernel,
        grid=(t // tm, n // tn),
        in_specs=[pl.BlockSpec((tm, k), lambda i, j: (i, 0)), w_spec, w_spec],
        out_specs=pl.BlockSpec((tm, tn), lambda i, j: (i, j)),
        out_shape=jax.ShapeDtypeStruct((t, n), BF16),
        compiler_params=_params("parallel", "arbitrary"),
        name="ffn_in",
    )(x, wg, wu)


def _ple_kernel(p_ref, h_ref, wp_ref, wg_ref, gple_ref, gnext_ref, hout_ref, anext_ref):
    h = h_ref[...]
    e = jnp.dot(p_ref[...].astype(BF16), wp_ref[...], preferred_element_type=F32)
    gate = _sigmoid(jnp.dot(h.astype(BF16), wg_ref[...], preferred_element_type=F32))
    _residual_epilogue(e * gate, h, gple_ref[...], gnext_ref[...], hout_ref, anext_ref)


def _ple(p, h, wp, wg, gple, gnext, *, tm):
    t, d = h.shape
    pd = p.shape[1]
    row_spec = pl.BlockSpec((tm, d), lambda i: (i, 0))
    vec_spec = pl.BlockSpec((1, d), lambda i: (0, 0))
    return pl.pallas_call(
        _ple_kernel,
        grid=(t // tm,),
        in_specs=[pl.BlockSpec((tm, pd), lambda i: (i, 0)), row_spec,
                  pl.BlockSpec(wp.shape, lambda i: (0, 0)), pl.BlockSpec(wg.shape, lambda i: (0, 0)),
                  vec_spec, vec_spec],
        out_specs=[row_spec, row_spec],
        out_shape=[jax.ShapeDtypeStruct((t, d), F32), jax.ShapeDtypeStruct((t, d), BF16)],
        compiler_params=_params("parallel"),
        name="ple_gate",
    )(p, h, wp, wg, _row(gple), _row(gnext))


def kernel(x, p, hg_w_in, hg_lb_logits, hg_out_norm, hg_w_out, gm_w_in, gm_ln_g, gm_ln_b, gm_w_s, gm_b_s,
           gm_w_out, norm_mix_pre, norm_mix_post, norm_ffn_pre, norm_ffn_post, ffn_w_gate, ffn_w_up,
           ffn_w_down, ple_w_proj, ple_w_gate, ple_norm):
    batch, seq, d = x.shape
    depth = p.shape[0]
    t = batch * seq
    fd = hg_lb_logits.shape[1]
    gw = gm_w_in.shape[2] // 2
    assert seq % HG_CHUNK == 0 and seq % GM_CHUNK == 0 and fd == d and gw == d

    lbs = _lower_bounds(hg_lb_logits)
    h = x.reshape(t, d)
    a = _prenorm(h, norm_mix_pre[0])
    ones = jnp.ones((d,), F32)

    for i in range(depth):
        j = i // 2
        if i % 2 == 0:
            w_in = hg_w_in[j].astype(BF16)
            (q,) = _mm(a, w_in, 0, fd, _ep_silu, [], [BF16], tm=1024, tn=1024, name="hgrn_in_q")
            log_f, kk = _mm(a, w_in, fd, fd, _ep_forget, [lbs[j].reshape(1, fd)], [F32, BF16],
                            tm=1024, tn=1024, name="hgrn_in_f")
            (v,) = _mm(a, w_in, 2 * fd, d, _ep_identity, [], [BF16], tm=1024, tn=1024, name="hgrn_in_v")
            (sg,) = _mm(a, w_in, 2 * fd + d, d, _ep_silu, [], [BF16], tm=1024, tn=1024, name="hgrn_in_g")
            y = _hgrn_recurrence(q, kk, log_f, v, sg, hg_out_norm[j], batch=batch, seq=seq)
            h, a = _mm_residual(y, hg_w_out[j].astype(BF16), h, norm_mix_post[i], norm_ffn_pre[i],
                                tm=512, tk=d, name="hgrn_out")
        else:
            w_in = gm_w_in[j].astype(BF16)
            (u,) = _mm(a, w_in, 0, gw, _ep_gelu, [], [BF16], tm=512, tn=gw, name="gmlp_in_u")
            (v,) = _mm(a, w_in, gw, gw, _ep_gelu_layernorm, [_row(gm_ln_g[j]), _row(gm_ln_b[j])], [BF16],
                       tm=512, tn=gw, name="gmlp_in_v")
            h, a = _sgu_out(u, v, gm_w_s[j].astype(BF16), gm_b_s[j].T.astype(F32), gm_w_out[j].astype(BF16), h,
                            norm_mix_post[i], norm_ffn_pre[i], tm=512)

        act = _ffn_in(a, ffn_w_gate[i].astype(BF16), ffn_w_up[i].astype(BF16), tm=1024, tn=512)
        h, _ = _mm_residual(act, ffn_w_down[i].astype(BF16), h, norm_ffn_post[i], ones,
                            tm=512, tk=512, name="ffn_out")
        gnext = norm_mix_pre[i + 1] if i + 1 < depth else ones
        h, a = _ple(p[i].reshape(t, -1), h, ple_w_proj[i].astype(BF16), ple_w_gate[i].astype(BF16),
                    ple_norm[i], gnext, tm=512)

    return h.reshape(batch, seq, d)
```

```python
import functools

import jax
import jax.numpy as jnp
from jax import lax
from jax.experimental import pallas as pl
from jax.experimental.pallas import tpu as pltpu

F32 = jnp.float32
BF16 = jnp.bfloat16

EPS = 1e-6
LB_FLOOR = 1e-30
HG_EXPAND = 128
HG_CHUNK = 64
HG_SUB = 16
GM_CHUNK = 128
GM_GROUPS = 16

V7X_VMEM_BYTES = 64 * 1024 * 1024
VMEM_LIMIT_BYTES = V7X_VMEM_BYTES - 8 * 1024 * 1024
SAFE_DECAY_RANGE = 60.0


def _params(*semantics):
    return pltpu.CompilerParams(dimension_semantics=semantics, vmem_limit_bytes=VMEM_LIMIT_BYTES)


def _rms(x, gain):
    return x * lax.rsqrt(jnp.mean(x * x, axis=-1, keepdims=True) + EPS) * gain


def _sigmoid(x):
    return 1.0 / (1.0 + jnp.exp(-x))


def _silu(x):
    return x * _sigmoid(x)


def _row(v):
    return v.reshape(1, -1).astype(F32)


def _lower_bounds_kernel(logits_ref, lb_ref):
    x = logits_ref[...]
    n = x.shape[0]
    m = jnp.max(x, axis=0, keepdims=True)
    e = jnp.exp(x - m)
    sm = e / jnp.sum(e, axis=0, keepdims=True)
    run = sm[0:1] - sm[0:1]
    rows = []
    for r in range(n):
        run = run + sm[r:r + 1]
        rows.append(run - sm[0:1])
    lb_ref[...] = jnp.concatenate(rows, axis=0)


def _lower_bounds(logits):
    return pl.pallas_call(
        _lower_bounds_kernel,
        out_shape=jax.ShapeDtypeStruct(logits.shape, F32),
        name="hgrn_lower_bounds",
    )(logits.astype(F32))


def _prenorm_kernel(x_ref, g_ref, o_ref):
    o_ref[...] = _rms(x_ref[...], g_ref[...]).astype(o_ref.dtype)


def _prenorm(x, gain, tm=512):
    t, d = x.shape
    return pl.pallas_call(
        _prenorm_kernel,
        grid=(t // tm,),
        in_specs=[pl.BlockSpec((tm, d), lambda i: (i, 0)), pl.BlockSpec((1, d), lambda i: (0, 0))],
        out_specs=pl.BlockSpec((tm, d), lambda i: (i, 0)),
        out_shape=jax.ShapeDtypeStruct((t, d), BF16),
        compiler_params=_params("parallel"),
        name="prenorm",
    )(x, _row(gain))


def _mm_kernel(*refs, epilogue, n_extra, n_out):
    x_ref, w_ref = refs[0], refs[1]
    extras = refs[2:2 + n_extra]
    outs = refs[2 + n_extra:2 + n_extra + n_out]
    z = jnp.dot(x_ref[...], w_ref[...], preferred_element_type=F32)
    res = epilogue(z, *[e[...] for e in extras])
    for o_ref, r in zip(outs, res):
        o_ref[...] = r.astype(o_ref.dtype)


def _mm(x, w, col0, n_cols, epilogue, extras, out_dtypes, *, tm, tn, name):
    t, k = x.shape
    assert col0 % tn == 0 and n_cols % tn == 0 and t % tm == 0
    jb0 = col0 // tn
    in_specs = [pl.BlockSpec((tm, k), lambda i, j: (i, 0)),
                pl.BlockSpec((k, tn), lambda i, j: (0, jb0 + j))]
    in_specs += [pl.BlockSpec((1, tn), lambda i, j: (0, j)) for _ in extras]
    return pl.pallas_call(
        functools.partial(_mm_kernel, epilogue=epilogue, n_extra=len(extras), n_out=len(out_dtypes)),
        grid=(t // tm, n_cols // tn),
        in_specs=in_specs,
        out_specs=[pl.BlockSpec((tm, tn), lambda i, j: (i, j)) for _ in out_dtypes],
        out_shape=[jax.ShapeDtypeStruct((t, n_cols), dt) for dt in out_dtypes],
        compiler_params=_params("parallel", "arbitrary"),
        name=name,
    )(x, w, *extras)


def _ep_silu(z):
    return (_silu(z),)


def _ep_identity(z):
    return (z,)


def _ep_forget(z, lb):
    e = jnp.exp(-jnp.abs(z))
    r = 1.0 / (1.0 + e)
    pos = z >= 0.0
    sig = jnp.where(pos, r, e * r)
    nsig = jnp.where(pos, e * r, r)
    one_m = 1.0 - lb
    log_f = jnp.log(jnp.maximum(lb, LB_FLOOR) + one_m * sig)
    return log_f, one_m * nsig


def _gelu_exact(z):
    return 0.5 * z * (1.0 + lax.erf(z * (2.0 ** -0.5)))


def _ep_gelu(z):
    return (_gelu_exact(z),)


def _ep_gelu_layernorm(z, g, b):
    y = _gelu_exact(z)
    mu = jnp.mean(y, axis=-1, keepdims=True)
    yc = y - mu
    y = yc * lax.rsqrt(jnp.mean(yc * yc, axis=-1, keepdims=True) + EPS)
    return (y * g + b,)


def _hgrn_rec_kernel(q_ref, k_ref, g_ref, v_ref, sg_ref, gain_ref, y_ref, b_scr, upd_scr, st_scr, *, hb, seq, cu):
    C, L, E = HG_CHUNK, HG_SUB, HG_EXPAND
    n_sub = C // L
    n_chunks = seq // C
    W = hb * E
    contract_last = (((1,), (1,)), ((), ()))
    contract_first = (((0,), (0,)), ((), ()))

    row = lax.broadcasted_iota(jnp.int32, (C, C), 0)
    col = lax.broadcasted_iota(jnp.int32, (C, C), 1)
    tril = (row >= col).astype(BF16)

    def pass1(ci, rmax):
        rows = [pl.ds(pl.multiple_of((ci * cu + u) * C, C), C) for u in range(cu)]
        gs, sums = [], []
        for u in range(cu):
            g = g_ref[rows[u], :]
            g1 = g.astype(BF16)
            r1 = g - g1.astype(F32)
            g2 = r1.astype(BF16)
            g3 = (r1 - g2.astype(F32)).astype(BF16)
            gs.append(g)
            sums.append(jnp.dot(tril, jnp.concatenate([g1, g2, g3], axis=1), preferred_element_type=F32))
        kdecs = []
        for u in range(cu):
            b = sums[u][:, :W] + sums[u][:, W:2 * W] + sums[u][:, 2 * W:]
            b_scr[rows[u], :] = b
            for j in range(n_sub):
                bstart = b[j * L:j * L + 1] - gs[u][j * L:j * L + 1]
                rmax = jnp.maximum(rmax, bstart - b[(j + 1) * L - 1:(j + 1) * L])
            kdecs.append((k_ref[rows[u], :].astype(F32) * jnp.exp(b[C - 1:C] - b)).astype(BF16))
        for u in range(cu):
            for h in range(hb):
                cols = slice(h * E, (h + 1) * E)
                upd_scr[ci * cu + u, h] = lax.dot_general(v_ref[rows[u], cols], kdecs[u][:, cols], contract_first,
                                                          preferred_element_type=F32)
        return rmax

    rmax = lax.fori_loop(0, n_chunks // cu, pass1, jnp.zeros((1, W), F32))
    safe = jnp.max(rmax) <= SAFE_DECAY_RANGE

    def scan(c, states):
        decay = jnp.exp(b_scr[pl.ds(c * C + C - 1, 1), :])
        new_states = []
        for h in range(hb):
            st_scr[c, h] = states[h].astype(BF16)
            new_states.append(states[h] * decay[:, h * E:(h + 1) * E] + upd_scr[c, h])
        return tuple(new_states)

    lax.fori_loop(0, n_chunks, scan, tuple(jnp.zeros((E, E), F32) for _ in range(hb)))

    def load(c, h):
        r0 = pl.multiple_of(c * C, C)
        rows, cols = pl.ds(r0, C), slice(h * E, (h + 1) * E)
        q = q_ref[rows, cols].astype(F32)
        k = k_ref[rows, cols].astype(F32)
        return rows, cols, q, k, v_ref[rows, cols], g_ref[rows, cols], b_scr[rows, cols]

    def finish(rows, cols, o):
        y = _rms(o, gain_ref[...]) * sg_ref[rows, cols].astype(F32)
        y_ref[rows, cols] = y.astype(y_ref.dtype)

    wr = lax.broadcasted_iota(jnp.int32, (C, n_sub * C), 0)
    wc = lax.broadcasted_iota(jnp.int32, (C, n_sub * C), 1)
    keep_wide = jnp.logical_and(wc >= (wr // L) * C, wc <= (wr // L) * C + wr)

    def fast_body(ci, carry):
        chunk_heads = [(ci * cu + u, h) for u in range(cu) for h in range(hb)]
        items = [load(c, h) for c, h in chunk_heads]
        inter, scores = [], []
        for (rows, cols, q, k, v, g, b), (c, h) in zip(items, chunk_heads):
            qb = (q * jnp.exp(b)).astype(BF16)
            inter.append(lax.dot_general(qb, st_scr[c, h], contract_last, preferred_element_type=F32))
            qf, kw = [], []
            for j in range(n_sub):
                lo, hi = j * L, (j + 1) * L
                bstart = b[lo:lo + 1] - g[lo:lo + 1]
                qf.append(q[lo:hi] * jnp.exp(b[lo:hi] - bstart))
                kw.append((k[:hi] * jnp.exp(bstart - b[:hi])).astype(BF16))
                if hi < C:
                    kw.append(jnp.zeros((C - hi, E), BF16))
            qf = jnp.concatenate(qf, axis=0).astype(BF16)
            kw = jnp.concatenate(kw, axis=0)
            scores.append(lax.dot_general(qf, kw, contract_last, preferred_element_type=F32))
        intra = []
        for (rows, cols, q, k, v, g, b), s in zip(items, scores):
            a = jnp.where(keep_wide, s, 0.0).astype(BF16)
            intra.append(jnp.dot(a, jnp.concatenate([v] * n_sub, axis=0), preferred_element_type=F32))
        for (rows, cols, q, k, v, g, b), o1, o2 in zip(items, inter, intra):
            finish(rows, cols, o1 + o2)
        return carry

    s_idx = lax.broadcasted_iota(jnp.int32, (L, 1), 0)

    def exact_body(c, carry):
        for h in range(hb):
            rows, cols, q, k, v, g, b = load(c, h)
            qb = (q * jnp.exp(b)).astype(BF16)
            o = lax.dot_general(qb, st_scr[c, h], contract_last, preferred_element_type=F32)
            pieces = []
            for j in range(n_sub):
                lo, hi = j * L, (j + 1) * L
                bstart = b[lo:lo + 1] - g[lo:lo + 1]
                qf = (q[lo:hi] * jnp.exp(b[lo:hi] - bstart)).astype(BF16)
                if j > 0:
                    kf = (k[:lo] * jnp.exp(bstart - b[:lo])).astype(BF16)
                    a = lax.dot_general(qf, kf, contract_last, preferred_element_type=F32)
                    oj = jnp.dot(a.astype(BF16), v[:lo], preferred_element_type=F32)
                else:
                    oj = jnp.zeros((L, E), F32)
                qj, kj, bj = q[lo:hi], k[lo:hi], b[lo:hi]
                vj = v[lo:hi].astype(F32)
                out_rows = []
                for t in range(L):
                    m = s_idx <= t
                    wdec = jnp.where(m, jnp.exp(jnp.where(m, bj[t:t + 1] - bj, 0.0)), 0.0)
                    a_col = jnp.sum(wdec * qj[t:t + 1] * kj, axis=-1, keepdims=True)
                    out_rows.append(jnp.sum(a_col * vj, axis=0, keepdims=True))
                pieces.append(oj + jnp.concatenate(out_rows, axis=0))
            finish(rows, cols, o + jnp.concatenate(pieces, axis=0))
        return carry

    @pl.when(safe)
    def _():
        lax.fori_loop(0, n_chunks // cu, fast_body, 0)

    @pl.when(jnp.logical_not(safe))
    def _():
        lax.fori_loop(0, n_chunks, exact_body, 0)


def _hgrn_recurrence(q, k, log_f, v, sg, gain, *, batch, seq, hb=2, cu=8):
    t, d = q.shape
    w = hb * HG_EXPAND
    n_chunks = seq // HG_CHUNK
    assert n_chunks % cu == 0
    spec = pl.BlockSpec((seq, w), lambda b, h: (b, h))
    return pl.pallas_call(
        functools.partial(_hgrn_rec_kernel, hb=hb, seq=seq, cu=cu),
        grid=(batch, d // w),
        in_specs=[spec, spec, spec, spec, spec, pl.BlockSpec((1, HG_EXPAND), lambda b, h: (0, 0))],
        out_specs=spec,
        out_shape=jax.ShapeDtypeStruct((t, d), BF16),
        scratch_shapes=[pltpu.VMEM((seq, w), F32),
                        pltpu.VMEM((n_chunks, hb, HG_EXPAND, HG_EXPAND), F32),
                        pltpu.VMEM((n_chunks, hb, HG_EXPAND, HG_EXPAND), BF16)],
        compiler_params=_params("parallel", "parallel"),
        name="hgrn_recurrence",
    )(q, k, log_f, v, sg, _row(gain))


def _residual_epilogue(m, h, gpost, gnext, hout_ref, anext_ref):
    hn = h + _rms(m, gpost)
    hout_ref[...] = hn
    anext_ref[...] = _rms(hn, gnext).astype(anext_ref.dtype)


def _mm_res_kernel(y_ref, w_ref, h_ref, gpost_ref, gnext_ref, hout_ref, anext_ref):
    m = jnp.dot(y_ref[...], w_ref[...], preferred_element_type=F32)
    _residual_epilogue(m, h_ref[...], gpost_ref[...], gnext_ref[...], hout_ref, anext_ref)


def _mm_residual(y, w, h, gpost, gnext, *, tm, name):
    t, k = y.shape
    d = w.shape[1]
    assert t % tm == 0
    row_spec = pl.BlockSpec((tm, d), lambda i: (i, 0))
    vec_spec = pl.BlockSpec((1, d), lambda i: (0, 0))
    return pl.pallas_call(
        _mm_res_kernel,
        grid=(t // tm,),
        in_specs=[pl.BlockSpec((tm, k), lambda i: (i, 0)), pl.BlockSpec((k, d), lambda i: (0, 0)),
                  row_spec, vec_spec, vec_spec],
        out_specs=[row_spec, row_spec],
        out_shape=[jax.ShapeDtypeStruct((t, d), F32), jax.ShapeDtypeStruct((t, d), BF16)],
        compiler_params=_params("parallel"),
        name=name,
    )(y, w, h, _row(gpost), _row(gnext))


def _mm_res_wide_kernel(y_ref, w_ref, h_ref, gpost_ref, hout_ref, f_scr, *, nj, tn):
    j = pl.program_id(1)
    f_scr[j] = jnp.dot(y_ref[...], w_ref[...], preferred_element_type=F32)

    @pl.when(j == nj - 1)
    def _():
        d = nj * tn
        ss = jnp.sum(f_scr[0] * f_scr[0], axis=-1, keepdims=True)
        for jj in range(1, nj):
            ss = ss + jnp.sum(f_scr[jj] * f_scr[jj], axis=-1, keepdims=True)
        inv = lax.rsqrt(ss * (1.0 / d) + EPS)
        for jj in range(nj):
            cols = slice(jj * tn, (jj + 1) * tn)
            hout_ref[:, cols] = h_ref[:, cols] + f_scr[jj] * inv * gpost_ref[:, cols]


def _mm_residual_wide(y, w, h, gpost, *, tm, tn, name):
    t, k = y.shape
    d = w.shape[1]
    nj = d // tn
    assert d % tn == 0 and t % tm == 0
    row_spec = pl.BlockSpec((tm, d), lambda i, j: (i, 0))
    return pl.pallas_call(
        functools.partial(_mm_res_wide_kernel, nj=nj, tn=tn),
        grid=(t // tm, nj),
        in_specs=[pl.BlockSpec((tm, k), lambda i, j: (i, 0)),
                  pl.BlockSpec((k, tn), lambda i, j: (0, j)),
                  row_spec, pl.BlockSpec((1, d), lambda i, j: (0, 0))],
        out_specs=row_spec,
        out_shape=jax.ShapeDtypeStruct((t, d), F32),
        scratch_shapes=[pltpu.VMEM((nj, tm, tn), F32)],
        compiler_params=_params("parallel", "arbitrary"),
        name=name,
    )(y, w, h, _row(gpost))


def _sgu_out_kernel(u_ref, v_ref, ws_ref, bst_ref, w_ref, h_ref, gpost_ref, gnext_ref,
                    hout_ref, anext_ref, y_scr, *, tm):
    P, G = GM_CHUNK, GM_GROUPS
    gd = u_ref.shape[1] // G
    r = lax.broadcasted_iota(jnp.int32, (P, P), 0)
    c = lax.broadcasted_iota(jnp.int32, (P, P), 1)
    causal = r >= c

    def chunk(ci, carry):
        r0 = pl.multiple_of(ci * P, P)
        for g in range(G):
            cols = slice(g * gd, (g + 1) * gd)
            wg = jnp.where(causal, ws_ref[g], jnp.zeros((), ws_ref.dtype))
            sv = jnp.dot(wg, v_ref[pl.ds(r0, P), cols], preferred_element_type=F32)
            sv = sv + bst_ref[:, g:g + 1]
            y = u_ref[pl.ds(r0, P), cols].astype(F32) * sv
            y_scr[pl.ds(r0, P), cols] = y.astype(y_scr.dtype)
        return carry

    lax.fori_loop(0, tm // P, chunk, 0)
    m = jnp.dot(y_scr[...], w_ref[...], preferred_element_type=F32)
    _residual_epilogue(m, h_ref[...], gpost_ref[...], gnext_ref[...], hout_ref, anext_ref)


def _sgu_out(u, v, ws, bs_t, w, h, gpost, gnext, *, tm):
    t, d = u.shape
    row_spec = pl.BlockSpec((tm, d), lambda i: (i, 0))
    vec_spec = pl.BlockSpec((1, d), lambda i: (0, 0))
    return pl.pallas_call(
        functools.partial(_sgu_out_kernel, tm=tm),
        grid=(t // tm,),
        in_specs=[row_spec, row_spec,
                  pl.BlockSpec(ws.shape, lambda i: (0, 0, 0)),
                  pl.BlockSpec(bs_t.shape, lambda i: (0, 0)),
                  pl.BlockSpec(w.shape, lambda i: (0, 0)),
                  row_spec, vec_spec, vec_spec],
        out_specs=[row_spec, row_spec],
        out_shape=[jax.ShapeDtypeStruct((t, d), F32), jax.ShapeDtypeStruct((t, d), BF16)],
        scratch_shapes=[pltpu.VMEM((tm, d), BF16)],
        compiler_params=_params("parallel"),
        name="gmlp_sgu_out",
    )(u, v, ws, bs_t, w, h, _row(gpost), _row(gnext))


def _ffn_in_kernel(x_ref, wg_ref, wu_ref, o_ref):
    x = x_ref[...]
    gate = jnp.dot(x, wg_ref[...], preferred_element_type=F32)
    up = jnp.dot(x, wu_ref[...], preferred_element_type=F32)
    o_ref[...] = (_silu(gate) * up).astype(o_ref.dtype)


def _ffn_in(x, wg, wu, *, tm, tn):
    t, k = x.shape
    n = wg.shape[1]
    assert n % tn == 0 and t % tm == 0
    w_spec = pl.BlockSpec((k, tn), lambda i, j: (0, j))
    return pl.pallas_call(
        _ffn_in_kernel,
        grid=(t // tm, n // tn),
        in_specs=[pl.BlockSpec((tm, k), lambda i, j: (i, 0)), w_spec, w_spec],
        out_specs=pl.BlockSpec((tm, tn), lambda i, j: (i, j)),
        out_shape=jax.ShapeDtypeStruct((t, n), BF16),
        compiler_params=_params("parallel", "arbitrary"),
        name="ffn_in",
    )(x, wg, wu)


def _ple_kernel(p_ref, h_ref, wp_ref, wg_ref, gple_ref, gnext_ref, hout_ref, anext_ref):
    h = h_ref[...]
    e = jnp.dot(p_ref[...].astype(BF16), wp_ref[...], preferred_element_type=F32)
    gate = _sigmoid(jnp.dot(h.astype(BF16), wg_ref[...], preferred_element_type=F32))
    _residual_epilogue(e * gate, h, gple_ref[...], gnext_ref[...], hout_ref, anext_ref)


def _ple(p, h, wp, wg, gple, gnext, *, tm):
    t, d = h.shape
    pd = p.shape[1]
    row_spec = pl.BlockSpec((tm, d), lambda i: (i, 0))
    vec_spec = pl.BlockSpec((1, d), lambda i: (0, 0))
    return pl.pallas_call(
        _ple_kernel,
        grid=(t // tm,),
        in_specs=[pl.BlockSpec((tm, pd), lambda i: (i, 0)), row_spec,
                  pl.BlockSpec(wp.shape, lambda i: (0, 0)), pl.BlockSpec(wg.shape, lambda i: (0, 0)),
                  vec_spec, vec_spec],
        out_specs=[row_spec, row_spec],
        out_shape=[jax.ShapeDtypeStruct((t, d), F32), jax.ShapeDtypeStruct((t, d), BF16)],
        compiler_params=_params("parallel"),
        name="ple_gate",
    )(p, h, wp, wg, _row(gple), _row(gnext))


def kernel(x, p, hg_w_in, hg_lb_logits, hg_out_norm, hg_w_out, gm_w_in, gm_ln_g, gm_ln_b, gm_w_s, gm_b_s,
           gm_w_out, norm_mix_pre, norm_mix_post, norm_ffn_pre, norm_ffn_post, ffn_w_gate, ffn_w_up,
           ffn_w_down, ple_w_proj, ple_w_gate, ple_norm):
    batch, seq, d = x.shape
    depth = p.shape[0]
    t = batch * seq
    fd = hg_lb_logits.shape[1]
    gw = gm_w_in.shape[2] // 2
    assert seq % HG_CHUNK == 0 and seq % GM_CHUNK == 0 and fd == d and gw == d

    lbs = _lower_bounds(hg_lb_logits)
    h = x.reshape(t, d)
    a = _prenorm(h, norm_mix_pre[0])
    ones = jnp.ones((d,), F32)

    for i in range(depth):
        j = i // 2
        if i % 2 == 0:
            w_in = hg_w_in[j].astype(BF16)
            (q,) = _mm(a, w_in, 0, fd, _ep_silu, [], [BF16], tm=1024, tn=1024, name="hgrn_in_q")
            log_f, kk = _mm(a, w_in, fd, fd, _ep_forget, [lbs[j].reshape(1, fd)], [F32, BF16],
                            tm=1024, tn=1024, name="hgrn_in_f")
            (v,) = _mm(a, w_in, 2 * fd, d, _ep_identity, [], [BF16], tm=1024, tn=1024, name="hgrn_in_v")
            (sg,) = _mm(a, w_in, 2 * fd + d, d, _ep_silu, [], [BF16], tm=1024, tn=1024, name="hgrn_in_g")
            y = _hgrn_recurrence(q, kk, log_f, v, sg, hg_out_norm[j], batch=batch, seq=seq)
            h, a = _mm_residual(y, hg_w_out[j].astype(BF16), h, norm_mix_post[i], norm_ffn_pre[i],
                                tm=512, name="hgrn_out")
        else:
            w_in = gm_w_in[j].astype(BF16)
            (u,) = _mm(a, w_in, 0, gw, _ep_gelu, [], [BF16], tm=512, tn=gw, name="gmlp_in_u")
            (v,) = _mm(a, w_in, gw, gw, _ep_gelu_layernorm, [_row(gm_ln_g[j]), _row(gm_ln_b[j])], [BF16],
                       tm=512, tn=gw, name="gmlp_in_v")
            h, a = _sgu_out(u, v, gm_w_s[j].astype(BF16), gm_b_s[j].T.astype(F32), gm_w_out[j].astype(BF16), h,
                            norm_mix_post[i], norm_ffn_pre[i], tm=512)

        act = _ffn_in(a, ffn_w_gate[i].astype(BF16), ffn_w_up[i].astype(BF16), tm=1024, tn=512)
        h = _mm_residual_wide(act, ffn_w_down[i].astype(BF16), h, norm_ffn_post[i], tm=512, tn=512, name="ffn_out")
        gnext = norm_mix_pre[i + 1] if i + 1 < depth else ones
        h, a = _ple(p[i].reshape(t, -1), h, ple_w_proj[i].astype(BF16), ple_w_gate[i].astype(BF16),
                    ple_norm[i], gnext, tm=512)

    return h.reshape(batch, seq, d)
```

```python
import functools
from typing import Any, Callable, NamedTuple

import jax
import jax.numpy as jnp
from jax import lax
from jax.experimental import pallas as pl
from jax.experimental.pallas import tpu as pltpu

F32 = jnp.float32
BF16 = jnp.bfloat16

EPS = 1e-6
LB_FLOOR = 1e-30
HG_EXPAND = 128
HG_CHUNK = 64
HG_SUB = 16
GM_CHUNK = 128
GM_GROUPS = 16

V7X_VMEM_BYTES = 64 * 1024 * 1024
VMEM_LIMIT_BYTES = V7X_VMEM_BYTES - 8 * 1024 * 1024
ROW_PIECE = 16
COL_PIECE = 512
SAFE_DECAY_RANGE = 60.0


def _params(*semantics):
    return pltpu.CompilerParams(dimension_semantics=semantics, vmem_limit_bytes=VMEM_LIMIT_BYTES)


def _rms(x, gain):
    return x * lax.rsqrt(jnp.mean(x * x, axis=-1, keepdims=True) + EPS) * gain


def _sigmoid(x):
    return 1.0 / (1.0 + jnp.exp(-x))


def _silu(x):
    return x * _sigmoid(x)


def _row(v):
    return v.reshape(1, -1).astype(F32)


class _Blocked(NamedTuple):
    array: Any
    block: tuple
    index: Callable
    resident: bool = False


def _const2(_):
    return (0, 0)


def _lagged_kernel(*refs, n_now, n_lag, n_out, n_stash, n_chunks, stage1, stage2):
    now = refs[:n_now]
    lag = refs[n_now:n_now + n_lag]
    outs = refs[n_now + n_lag:n_now + n_lag + n_out]
    scratch = refs[n_now + n_lag + n_out:]
    ping, pong, extra = scratch[:n_stash], scratch[n_stash:2 * n_stash], scratch[2 * n_stash:]
    s = pl.program_id(0)

    @pl.when(s == 0)
    def _():
        for r in pong:
            r[...] = jnp.zeros(r.shape, r.dtype)

    def step(read, write):
        for c in range(n_chunks):
            stage1(now, write, extra, c, n_chunks)
            stage2(read, lag, outs, c, n_chunks)

    @pl.when(s % 2 == 0)
    def _():
        step(pong, ping)

    @pl.when(s % 2 == 1)
    def _():
        step(ping, pong)


def _lagged_call(stage1, stage2, now, lag, outs, stash, n_tiles, *, n_chunks, extra_scratch=(), name):
    last = n_tiles - 1

    def spec(op, lagged):
        if lagged:
            index = lambda s, f=op.index: f(jnp.maximum(s - 1, 0))
        else:
            index = lambda s, f=op.index: f(jnp.minimum(s, last))
        if op.resident:
            return pl.BlockSpec(op.block, index, pipeline_mode=pl.Buffered(1))
        return pl.BlockSpec(op.block, index)

    return pl.pallas_call(
        functools.partial(_lagged_kernel, n_now=len(now), n_lag=len(lag), n_out=len(outs), n_stash=len(stash),
                          n_chunks=n_chunks, stage1=stage1, stage2=stage2),
        grid=(n_tiles + 1,),
        in_specs=[spec(op, False) for op in now] + [spec(op, True) for op in lag],
        out_specs=[spec(op, True) for op in outs],
        out_shape=[op.array for op in outs],
        scratch_shapes=[pltpu.VMEM(shape, F32) for shape in stash] * 2 + list(extra_scratch),
        compiler_params=_params("arbitrary"),
        name=name,
    )(*[op.array for op in now], *[op.array for op in lag])


def _lower_bounds_kernel(logits_ref, lb_ref):
    x = logits_ref[...]
    n = x.shape[0]
    m = jnp.max(x, axis=0, keepdims=True)
    e = jnp.exp(x - m)
    sm = e / jnp.sum(e, axis=0, keepdims=True)
    run = sm[0:1] - sm[0:1]
    rows = []
    for r in range(n):
        run = run + sm[r:r + 1]
        rows.append(run - sm[0:1])
    lb_ref[...] = jnp.concatenate(rows, axis=0)


def _lower_bounds(logits):
    return pl.pallas_call(
        _lower_bounds_kernel,
        out_shape=jax.ShapeDtypeStruct(logits.shape, F32),
        name="hgrn_lower_bounds",
    )(logits.astype(F32))


def _prenorm_kernel(x_ref, g_ref, o_ref):
    o_ref[...] = _rms(x_ref[...], g_ref[...]).astype(o_ref.dtype)


def _prenorm(x, gain, tm=512):
    t, d = x.shape
    return pl.pallas_call(
        _prenorm_kernel,
        grid=(t // tm,),
        in_specs=[pl.BlockSpec((tm, d), lambda i: (i, 0)), pl.BlockSpec((1, d), lambda i: (0, 0))],
        out_specs=pl.BlockSpec((tm, d), lambda i: (i, 0)),
        out_shape=jax.ShapeDtypeStruct((t, d), BF16),
        compiler_params=_params("parallel"),
        name="prenorm",
    )(x, _row(gain))


def _mm_kernel(*refs, epilogue, n_extra, n_out):
    x_ref, w_ref = refs[0], refs[1]
    extras = refs[2:2 + n_extra]
    outs = refs[2 + n_extra:2 + n_extra + n_out]
    z = jnp.dot(x_ref[...], w_ref[...], preferred_element_type=F32)
    res = epilogue(z, *[e[...] for e in extras])
    for o_ref, r in zip(outs, res):
        o_ref[...] = r.astype(o_ref.dtype)


def _mm(x, w, col0, n_cols, epilogue, extras, out_dtypes, *, tm, tn, name):
    t, k = x.shape
    assert col0 % tn == 0 and n_cols % tn == 0 and t % tm == 0
    jb0 = col0 // tn
    in_specs = [pl.BlockSpec((tm, k), lambda i, j: (i, 0)),
                pl.BlockSpec((k, tn), lambda i, j: (0, jb0 + j))]
    in_specs += [pl.BlockSpec((1, tn), lambda i, j: (0, j)) for _ in extras]
    return pl.pallas_call(
        functools.partial(_mm_kernel, epilogue=epilogue, n_extra=len(extras), n_out=len(out_dtypes)),
        grid=(t // tm, n_cols // tn),
        in_specs=in_specs,
        out_specs=[pl.BlockSpec((tm, tn), lambda i, j: (i, j)) for _ in out_dtypes],
        out_shape=[jax.ShapeDtypeStruct((t, n_cols), dt) for dt in out_dtypes],
        compiler_params=_params("parallel", "arbitrary"),
        name=name,
    )(x, w, *extras)


def _ep_silu(z):
    return (_silu(z),)


def _ep_identity(z):
    return (z,)


def _ep_forget(z, lb):
    e = jnp.exp(-jnp.abs(z))
    r = 1.0 / (1.0 + e)
    pos = z >= 0.0
    sig = jnp.where(pos, r, e * r)
    nsig = jnp.where(pos, e * r, r)
    one_m = 1.0 - lb
    log_f = jnp.log(jnp.maximum(lb, LB_FLOOR) + one_m * sig)
    return log_f, one_m * nsig


def _gelu_exact(z):
    return 0.5 * z * (1.0 + lax.erf(z * (2.0 ** -0.5)))


def _ep_gelu(z):
    return (_gelu_exact(z),)


def _hgrn_rec_kernel(q_ref, k_ref, g_ref, v_ref, sg_ref, gain_ref, y_ref, b_scr, upd_scr, st_scr, *, hb, seq, cu):
    C, L, E = HG_CHUNK, HG_SUB, HG_EXPAND
    n_sub = C // L
    n_chunks = seq // C
    W = hb * E
    contract_last = (((1,), (1,)), ((), ()))
    contract_first = (((0,), (0,)), ((), ()))

    row = lax.broadcasted_iota(jnp.int32, (C, C), 0)
    col = lax.broadcasted_iota(jnp.int32, (C, C), 1)
    tril = (row >= col).astype(BF16)

    def pass1(ci, rmax):
        rows = [pl.ds(pl.multiple_of((ci * cu + u) * C, C), C) for u in range(cu)]
        gs, sums = [], []
        for u in range(cu):
            g = g_ref[rows[u], :]
            g1 = g.astype(BF16)
            r1 = g - g1.astype(F32)
            g2 = r1.astype(BF16)
            g3 = (r1 - g2.astype(F32)).astype(BF16)
            gs.append(g)
            sums.append(jnp.dot(tril, jnp.concatenate([g1, g2, g3], axis=1), preferred_element_type=F32))
        kdecs = []
        for u in range(cu):
            b = sums[u][:, :W] + sums[u][:, W:2 * W] + sums[u][:, 2 * W:]
            b_scr[rows[u], :] = b
            for j in range(n_sub):
                bstart = b[j * L:j * L + 1] - gs[u][j * L:j * L + 1]
                rmax = jnp.maximum(rmax, bstart - b[(j + 1) * L - 1:(j + 1) * L])
            kdecs.append((k_ref[rows[u], :].astype(F32) * jnp.exp(b[C - 1:C] - b)).astype(BF16))
        for u in range(cu):
            for h in range(hb):
                cols = slice(h * E, (h + 1) * E)
                upd_scr[ci * cu + u, h] = lax.dot_general(v_ref[rows[u], cols], kdecs[u][:, cols], contract_first,
                                                          preferred_element_type=F32)
        return rmax

    rmax = lax.fori_loop(0, n_chunks // cu, pass1, jnp.zeros((1, W), F32))
    safe = jnp.max(rmax) <= SAFE_DECAY_RANGE

    def scan(c, states):
        decay = jnp.exp(b_scr[pl.ds(c * C + C - 1, 1), :])
        new_states = []
        for h in range(hb):
            st_scr[c, h] = states[h].astype(BF16)
            new_states.append(states[h] * decay[:, h * E:(h + 1) * E] + upd_scr[c, h])
        return tuple(new_states)

    lax.fori_loop(0, n_chunks, scan, tuple(jnp.zeros((E, E), F32) for _ in range(hb)))

    def load(c, h):
        r0 = pl.multiple_of(c * C, C)
        rows, cols = pl.ds(r0, C), slice(h * E, (h + 1) * E)
        q = q_ref[rows, cols].astype(F32)
        k = k_ref[rows, cols].astype(F32)
        return rows, cols, q, k, v_ref[rows, cols], g_ref[rows, cols], b_scr[rows, cols]

    def finish(rows, cols, o):
        y = _rms(o, gain_ref[...]) * sg_ref[rows, cols].astype(F32)
        y_ref[rows, cols] = y.astype(y_ref.dtype)

    wr = lax.broadcasted_iota(jnp.int32, (C, n_sub * C), 0)
    wc = lax.broadcasted_iota(jnp.int32, (C, n_sub * C), 1)
    keep_wide = jnp.logical_and(wc >= (wr // L) * C, wc <= (wr // L) * C + wr)

    def fast_body(ci, carry):
        chunk_heads = [(ci * cu + u, h) for u in range(cu) for h in range(hb)]
        items = [load(c, h) for c, h in chunk_heads]
        inter, scores = [], []
        for (rows, cols, q, k, v, g, b), (c, h) in zip(items, chunk_heads):
            qb = (q * jnp.exp(b)).astype(BF16)
            inter.append(lax.dot_general(qb, st_scr[c, h], contract_last, preferred_element_type=F32))
            qf, kw = [], []
            for j in range(n_sub):
                lo, hi = j * L, (j + 1) * L
                bstart = b[lo:lo + 1] - g[lo:lo + 1]
                qf.append(q[lo:hi] * jnp.exp(b[lo:hi] - bstart))
                kw.append((k[:hi] * jnp.exp(bstart - b[:hi])).astype(BF16))
                if hi < C:
                    kw.append(jnp.zeros((C - hi, E), BF16))
            qf = jnp.concatenate(qf, axis=0).astype(BF16)
            kw = jnp.concatenate(kw, axis=0)
            scores.append(lax.dot_general(qf, kw, contract_last, preferred_element_type=F32))
        intra = []
        for (rows, cols, q, k, v, g, b), s in zip(items, scores):
            a = jnp.where(keep_wide, s, 0.0).astype(BF16)
            intra.append(jnp.dot(a, jnp.concatenate([v] * n_sub, axis=0), preferred_element_type=F32))
        for (rows, cols, q, k, v, g, b), o1, o2 in zip(items, inter, intra):
            finish(rows, cols, o1 + o2)
        return carry

    s_idx = lax.broadcasted_iota(jnp.int32, (L, 1), 0)

    def exact_body(c, carry):
        for h in range(hb):
            rows, cols, q, k, v, g, b = load(c, h)
            qb = (q * jnp.exp(b)).astype(BF16)
            o = lax.dot_general(qb, st_scr[c, h], contract_last, preferred_element_type=F32)
            pieces = []
            for j in range(n_sub):
                lo, hi = j * L, (j + 1) * L
                bstart = b[lo:lo + 1] - g[lo:lo + 1]
                qf = (q[lo:hi] * jnp.exp(b[lo:hi] - bstart)).astype(BF16)
                if j > 0:
                    kf = (k[:lo] * jnp.exp(bstart - b[:lo])).astype(BF16)
                    a = lax.dot_general(qf, kf, contract_last, preferred_element_type=F32)
                    oj = jnp.dot(a.astype(BF16), v[:lo], preferred_element_type=F32)
                else:
                    oj = jnp.zeros((L, E), F32)
                qj, kj, bj = q[lo:hi], k[lo:hi], b[lo:hi]
                vj = v[lo:hi].astype(F32)
                out_rows = []
                for t in range(L):
                    m = s_idx <= t
                    wdec = jnp.where(m, jnp.exp(jnp.where(m, bj[t:t + 1] - bj, 0.0)), 0.0)
                    a_col = jnp.sum(wdec * qj[t:t + 1] * kj, axis=-1, keepdims=True)
                    out_rows.append(jnp.sum(a_col * vj, axis=0, keepdims=True))
                pieces.append(oj + jnp.concatenate(out_rows, axis=0))
            finish(rows, cols, o + jnp.concatenate(pieces, axis=0))
        return carry

    @pl.when(safe)
    def _():
        lax.fori_loop(0, n_chunks // cu, fast_body, 0)

    @pl.when(jnp.logical_not(safe))
    def _():
        lax.fori_loop(0, n_chunks, exact_body, 0)


def _hgrn_recurrence(q, k, log_f, v, sg, gain, *, batch, seq, hb=2, cu=8):
    t, d = q.shape
    w = hb * HG_EXPAND
    n_chunks = seq // HG_CHUNK
    assert n_chunks % cu == 0
    spec = pl.BlockSpec((seq, w), lambda b, h: (b, h))
    return pl.pallas_call(
        functools.partial(_hgrn_rec_kernel, hb=hb, seq=seq, cu=cu),
        grid=(batch, d // w),
        in_specs=[spec, spec, spec, spec, spec, pl.BlockSpec((1, HG_EXPAND), lambda b, h: (0, 0))],
        out_specs=spec,
        out_shape=jax.ShapeDtypeStruct((t, d), BF16),
        scratch_shapes=[pltpu.VMEM((seq, w), F32),
                        pltpu.VMEM((n_chunks, hb, HG_EXPAND, HG_EXPAND), F32),
                        pltpu.VMEM((n_chunks, hb, HG_EXPAND, HG_EXPAND), BF16)],
        compiler_params=_params("parallel", "parallel"),
        name="hgrn_recurrence",
    )(q, k, log_f, v, sg, _row(gain))


def _rows(i):
    return (i, 0)


def _chunk(c, n_chunks, size):
    step = size // n_chunks
    return slice(c * step, (c + 1) * step)


def _pieces(ref, c, n_chunks):
    chunk = _chunk(c, n_chunks, ref.shape[0])
    rows = [slice(r0, r0 + ROW_PIECE) for r0 in range(chunk.start, chunk.stop, ROW_PIECE)]
    cols = [slice(c0, c0 + COL_PIECE) for c0 in range(0, ref.shape[1], COL_PIECE)]
    return rows, cols


def _residual_stage2(read, lag, outs, c, n_chunks):
    (m_ref,), (h_ref, gpost_ref, gnext_ref), (hout_ref, anext_ref) = read, lag, outs
    d = m_ref.shape[1]
    row_pieces, col_blocks = _pieces(m_ref, c, n_chunks)
    for rows in row_pieces:
        ss = sum(jnp.sum(jnp.square(m_ref[rows, cb]), axis=-1, keepdims=True) for cb in col_blocks)
        inv = lax.rsqrt(ss * (1.0 / d) + EPS)
        ss = 0.0
        for cb in col_blocks:
            hn = h_ref[rows, cb] + m_ref[rows, cb] * inv * gpost_ref[:, cb]
            hout_ref[rows, cb] = hn
            ss = ss + jnp.sum(jnp.square(hn), axis=-1, keepdims=True)
        if gnext_ref is None:
            for cb in col_blocks:
                anext_ref[rows, cb] = hout_ref[rows, cb].astype(anext_ref.dtype)
        else:
            inv = lax.rsqrt(ss * (1.0 / d) + EPS)
            for cb in col_blocks:
                anext_ref[rows, cb] = (hout_ref[rows, cb] * inv * gnext_ref[:, cb]).astype(anext_ref.dtype)


def _residual_plain_stage2(read, lag, outs, c, n_chunks):
    (h_ref, gpost_ref), (hout_ref, hb_ref) = lag, outs
    _residual_stage2(read, (h_ref, gpost_ref, None), (hout_ref, hb_ref), c, n_chunks)


def _dot_stage1(now, write, extra, c, n_chunks):
    (x_ref, w_ref), (m_ref,) = now, write
    cols = _chunk(c, n_chunks, w_ref.shape[1])
    m_ref[:, cols] = jnp.dot(x_ref[...], w_ref[:, cols], preferred_element_type=F32)


def _mm_residual(y, w, h, gpost, gnext, *, tm, name):
    t, k = y.shape
    d = w.shape[1]
    assert t % tm == 0
    lag = [_Blocked(h, (tm, d), _rows), _Blocked(_row(gpost), (1, d), _const2)]
    if gnext is not None:
        lag.append(_Blocked(_row(gnext), (1, d), _const2))
    return _lagged_call(
        _dot_stage1, _residual_plain_stage2 if gnext is None else _residual_stage2,
        now=[_Blocked(y, (tm, k), _rows), _Blocked(w, (k, d), _const2, resident=True)],
        lag=lag,
        outs=[_Blocked(jax.ShapeDtypeStruct((t, d), F32), (tm, d), _rows),
              _Blocked(jax.ShapeDtypeStruct((t, d), BF16), (tm, d), _rows)],
        stash=[(tm, d)], n_tiles=t // tm, n_chunks=8, name=name)


def _sgu_stage1(now, write, extra, c, n_chunks):
    (u_ref, v_ref, ws_ref, bst_ref, w_ref), (m_ref,), (y_scr,) = now, write, extra
    if c == 0:
        P, G = GM_CHUNK, GM_GROUPS
        gd = u_ref.shape[1] // G
        causal = lax.broadcasted_iota(jnp.int32, (P, P), 0) >= lax.broadcasted_iota(jnp.int32, (P, P), 1)

        def chunk(ci, carry):
            r0 = pl.multiple_of(ci * P, P)
            for g in range(G):
                cols = slice(g * gd, (g + 1) * gd)
                wg = jnp.where(causal, ws_ref[g], jnp.zeros((), ws_ref.dtype))
                sv = jnp.dot(wg, v_ref[pl.ds(r0, P), cols], preferred_element_type=F32)
                sv = sv + bst_ref[:, g:g + 1]
                y = u_ref[pl.ds(r0, P), cols].astype(F32) * sv
                y_scr[pl.ds(r0, P), cols] = y.astype(y_scr.dtype)
            return carry

        lax.fori_loop(0, u_ref.shape[0] // P, chunk, 0)
    cols = _chunk(c, n_chunks, w_ref.shape[1])
    m_ref[:, cols] = jnp.dot(y_scr[...], w_ref[:, cols], preferred_element_type=F32)


def _sgu_out(u, v, ws, bs_t, w, h, gpost, gnext, *, tm):
    t, d = u.shape
    return _lagged_call(
        _sgu_stage1, _residual_stage2,
        now=[_Blocked(u, (tm, d), _rows), _Blocked(v, (tm, d), _rows),
             _Blocked(ws, ws.shape, lambda i: (0, 0, 0)), _Blocked(bs_t, bs_t.shape, _const2),
             _Blocked(w, w.shape, _const2, resident=True)],
        lag=[_Blocked(h, (tm, d), _rows), _Blocked(_row(gpost), (1, d), _const2), _Blocked(_row(gnext), (1, d), _const2)],
        outs=[_Blocked(jax.ShapeDtypeStruct((t, d), F32), (tm, d), _rows),
              _Blocked(jax.ShapeDtypeStruct((t, d), BF16), (tm, d), _rows)],
        stash=[(tm, d)], n_tiles=t // tm, n_chunks=8,
        extra_scratch=[pltpu.VMEM((tm, d), BF16)], name="gmlp_sgu_out")


def _gelu_layernorm_stage2(read, lag, outs, c, n_chunks):
    (z_ref,), (g_ref, b_ref), (o_ref,) = read, lag, outs
    d = z_ref.shape[1]
    row_pieces, col_blocks = _pieces(z_ref, c, n_chunks)
    for rows in row_pieces:
        s1 = 0.0
        for cb in col_blocks:
            y = _gelu_exact(z_ref[rows, cb])
            z_ref[rows, cb] = y
            s1 = s1 + jnp.sum(y, axis=-1, keepdims=True)
        mu = s1 * (1.0 / d)
        s2 = sum(jnp.sum(jnp.square(z_ref[rows, cb] - mu), axis=-1, keepdims=True) for cb in col_blocks)
        inv = lax.rsqrt(s2 * (1.0 / d) + EPS)
        for cb in col_blocks:
            o_ref[rows, cb] = ((z_ref[rows, cb] - mu) * inv * g_ref[:, cb] + b_ref[:, cb]).astype(o_ref.dtype)


def _mm_gelu_layernorm(x, w, col_block, g, b, *, tm, name):
    t, k = x.shape
    d = g.shape[0]
    return _lagged_call(
        _dot_stage1, _gelu_layernorm_stage2,
        now=[_Blocked(x, (tm, k), _rows), _Blocked(w, (k, d), lambda i: (0, col_block), resident=True)],
        lag=[_Blocked(_row(g), (1, d), _const2), _Blocked(_row(b), (1, d), _const2)],
        outs=[_Blocked(jax.ShapeDtypeStruct((t, d), BF16), (tm, d), _rows)],
        stash=[(tm, d)], n_tiles=t // tm, n_chunks=8, name=name)


def _ffn_in_kernel(x_ref, wg_ref, wu_ref, o_ref):
    x = x_ref[...]
    gate = jnp.dot(x, wg_ref[...], preferred_element_type=F32)
    up = jnp.dot(x, wu_ref[...], preferred_element_type=F32)
    o_ref[...] = (_silu(gate) * up).astype(o_ref.dtype)


def _ffn_in(x, wg, wu, *, tm, tn):
    t, k = x.shape
    n = wg.shape[1]
    assert n % tn == 0 and t % tm == 0
    w_spec = pl.BlockSpec((k, tn), lambda i, j: (0, j))
    return pl.pallas_call(
        _ffn_in_kernel,
        grid=(t // tm, n // tn),
        in_specs=[pl.BlockSpec((tm, k), lambda i, j: (i, 0)), w_spec, w_spec],
        out_specs=pl.BlockSpec((tm, tn), lambda i, j: (i, j)),
        out_shape=jax.ShapeDtypeStruct((t, n), BF16),
        compiler_params=_params("parallel", "arbitrary"),
        name="ffn_in",
    )(x, wg, wu)


def _ple_stage1(now, write, extra, c, n_chunks):
    (p_ref, hb_ref, wp_ref, wg_ref), (z_ref,) = now, write
    cols = _chunk(c, n_chunks, wg_ref.shape[1])
    e = jnp.dot(p_ref[...].astype(BF16), wp_ref[:, cols], preferred_element_type=F32)
    gate = _sigmoid(jnp.dot(hb_ref[...], wg_ref[:, cols], preferred_element_type=F32))
    z_ref[:, cols] = e * gate


def _ple(p, h, hb, wp, wg, gple, gnext, *, tm):
    t, d = h.shape
    pd = p.shape[1]
    return _lagged_call(
        _ple_stage1, _residual_stage2,
        now=[_Blocked(p, (tm, pd), _rows), _Blocked(hb, (tm, d), _rows),
             _Blocked(wp, wp.shape, _const2, resident=True), _Blocked(wg, wg.shape, _const2, resident=True)],
        lag=[_Blocked(h, (tm, d), _rows), _Blocked(_row(gple), (1, d), _const2), _Blocked(_row(gnext), (1, d), _const2)],
        outs=[_Blocked(jax.ShapeDtypeStruct((t, d), F32), (tm, d), _rows),
              _Blocked(jax.ShapeDtypeStruct((t, d), BF16), (tm, d), _rows)],
        stash=[(tm, d)], n_tiles=t // tm, n_chunks=8, name="ple_gate")


def kernel(x, p, hg_w_in, hg_lb_logits, hg_out_norm, hg_w_out, gm_w_in, gm_ln_g, gm_ln_b, gm_w_s, gm_b_s,
           gm_w_out, norm_mix_pre, norm_mix_post, norm_ffn_pre, norm_ffn_post, ffn_w_gate, ffn_w_up,
           ffn_w_down, ple_w_proj, ple_w_gate, ple_norm):
    batch, seq, d = x.shape
    depth = p.shape[0]
    t = batch * seq
    fd = hg_lb_logits.shape[1]
    gw = gm_w_in.shape[2] // 2
    assert seq % HG_CHUNK == 0 and seq % GM_CHUNK == 0 and fd == d and gw == d

    lbs = _lower_bounds(hg_lb_logits)
    h = x.reshape(t, d)
    a = _prenorm(h, norm_mix_pre[0])
    ones = jnp.ones((d,), F32)

    for i in range(depth):
        j = i // 2
        if i % 2 == 0:
            w_in = hg_w_in[j].astype(BF16)
            (q,) = _mm(a, w_in, 0, fd, _ep_silu, [], [BF16], tm=1024, tn=1024, name="hgrn_in_q")
            log_f, kk = _mm(a, w_in, fd, fd, _ep_forget, [lbs[j].reshape(1, fd)], [F32, BF16],
                            tm=1024, tn=1024, name="hgrn_in_f")
            (v,) = _mm(a, w_in, 2 * fd, d, _ep_identity, [], [BF16], tm=1024, tn=1024, name="hgrn_in_v")
            (sg,) = _mm(a, w_in, 2 * fd + d, d, _ep_silu, [], [BF16], tm=1024, tn=1024, name="hgrn_in_g")
            y = _hgrn_recurrence(q, kk, log_f, v, sg, hg_out_norm[j], batch=batch, seq=seq)
            h, a = _mm_residual(y, hg_w_out[j].astype(BF16), h, norm_mix_post[i], norm_ffn_pre[i],
                                tm=512, name="hgrn_out")
        else:
            w_in = gm_w_in[j].astype(BF16)
            (u,) = _mm(a, w_in, 0, gw, _ep_gelu, [], [BF16], tm=512, tn=gw, name="gmlp_in_u")
            (v,) = _mm_gelu_layernorm(a, w_in, 1, gm_ln_g[j], gm_ln_b[j], tm=512, name="gmlp_in_v")
            h, a = _sgu_out(u, v, gm_w_s[j].astype(BF16), gm_b_s[j].T.astype(F32), gm_w_out[j].astype(BF16), h,
                            norm_mix_post[i], norm_ffn_pre[i], tm=512)

        act = _ffn_in(a, ffn_w_gate[i].astype(BF16), ffn_w_up[i].astype(BF16), tm=1024, tn=512)
        h, hb = _mm_residual(act, ffn_w_down[i].astype(BF16), h, norm_ffn_post[i], None, tm=256, name="ffn_out")
        gnext = norm_mix_pre[i + 1] if i + 1 < depth else ones
        h, a = _ple(p[i].reshape(t, -1), h, hb, ple_w_proj[i].astype(BF16), ple_w_gate[i].astype(BF16),
                    ple_norm[i], gnext, tm=512)

    return h.reshape(batch, seq, d)
```

```python
import functools
from typing import Any, Callable, NamedTuple

import jax
import jax.numpy as jnp
from jax import lax
from jax.experimental import pallas as pl
from jax.experimental.pallas import tpu as pltpu

F32 = jnp.float32
BF16 = jnp.bfloat16

EPS = 1e-6
LB_FLOOR = 1e-30
HG_EXPAND = 128
HG_CHUNK = 64
HG_SUB = 16
GM_CHUNK = 128
GM_GROUPS = 16

V7X_VMEM_BYTES = 64 * 1024 * 1024
VMEM_LIMIT_BYTES = V7X_VMEM_BYTES - 8 * 1024 * 1024
ROW_PIECE = 16
COL_PIECE = 256
SAFE_DECAY_RANGE = 60.0


def _params(*semantics):
    return pltpu.CompilerParams(dimension_semantics=semantics, vmem_limit_bytes=VMEM_LIMIT_BYTES)


def _rms(x, gain):
    return x * lax.rsqrt(jnp.mean(x * x, axis=-1, keepdims=True) + EPS) * gain


def _sigmoid(x):
    return 1.0 / (1.0 + jnp.exp(-x))


def _silu(x):
    return x * _sigmoid(x)


def _row(v):
    return v.reshape(1, -1).astype(F32)


class _Blocked(NamedTuple):
    array: Any
    block: tuple
    index: Callable
    resident: bool = False


def _const2(_):
    return (0, 0)


def _lagged_kernel(*refs, n_now, n_lag, n_out, n_stash, n_chunks, stage1, stage2):
    now = refs[:n_now]
    lag = refs[n_now:n_now + n_lag]
    outs = refs[n_now + n_lag:n_now + n_lag + n_out]
    scratch = refs[n_now + n_lag + n_out:]
    ping, pong, extra = scratch[:n_stash], scratch[n_stash:2 * n_stash], scratch[2 * n_stash:]
    s = pl.program_id(0)

    @pl.when(s == 0)
    def _():
        for r in pong:
            r[...] = jnp.zeros(r.shape, r.dtype)

    def step(read, write):
        for c in range(n_chunks):
            stage1(now, write, extra, c, n_chunks)
            stage2(read, lag, outs, c, n_chunks)

    @pl.when(s % 2 == 0)
    def _():
        step(pong, ping)

    @pl.when(s % 2 == 1)
    def _():
        step(ping, pong)


def _lagged_call(stage1, stage2, now, lag, outs, stash, n_tiles, *, n_chunks, extra_scratch=(), name):
    last = n_tiles - 1

    def spec(op, lagged):
        if lagged:
            index = lambda s, f=op.index: f(jnp.maximum(s - 1, 0))
        else:
            index = lambda s, f=op.index: f(jnp.minimum(s, last))
        if op.resident:
            return pl.BlockSpec(op.block, index, pipeline_mode=pl.Buffered(1))
        return pl.BlockSpec(op.block, index)

    return pl.pallas_call(
        functools.partial(_lagged_kernel, n_now=len(now), n_lag=len(lag), n_out=len(outs), n_stash=len(stash),
                          n_chunks=n_chunks, stage1=stage1, stage2=stage2),
        grid=(n_tiles + 1,),
        in_specs=[spec(op, False) for op in now] + [spec(op, True) for op in lag],
        out_specs=[spec(op, True) for op in outs],
        out_shape=[op.array for op in outs],
        scratch_shapes=[pltpu.VMEM(shape, F32) for shape in stash] * 2 + list(extra_scratch),
        compiler_params=_params("arbitrary"),
        name=name,
    )(*[op.array for op in now], *[op.array for op in lag])


def _lower_bounds_kernel(logits_ref, lb_ref):
    x = logits_ref[...]
    n = x.shape[0]
    m = jnp.max(x, axis=0, keepdims=True)
    e = jnp.exp(x - m)
    sm = e / jnp.sum(e, axis=0, keepdims=True)
    run = sm[0:1] - sm[0:1]
    rows = []
    for r in range(n):
        run = run + sm[r:r + 1]
        rows.append(run - sm[0:1])
    lb_ref[...] = jnp.concatenate(rows, axis=0)


def _lower_bounds(logits):
    return pl.pallas_call(
        _lower_bounds_kernel,
        out_shape=jax.ShapeDtypeStruct(logits.shape, F32),
        name="hgrn_lower_bounds",
    )(logits.astype(F32))


def _prenorm_kernel(x_ref, g_ref, o_ref):
    o_ref[...] = _rms(x_ref[...], g_ref[...]).astype(o_ref.dtype)


def _prenorm(x, gain, tm=512):
    t, d = x.shape
    return pl.pallas_call(
        _prenorm_kernel,
        grid=(t // tm,),
        in_specs=[pl.BlockSpec((tm, d), lambda i: (i, 0)), pl.BlockSpec((1, d), lambda i: (0, 0))],
        out_specs=pl.BlockSpec((tm, d), lambda i: (i, 0)),
        out_shape=jax.ShapeDtypeStruct((t, d), BF16),
        compiler_params=_params("parallel"),
        name="prenorm",
    )(x, _row(gain))


def _mm_kernel(*refs, epilogue, n_w, n_extra, n_out):
    x_ref = refs[0]
    w_refs = refs[1:1 + n_w]
    extras = refs[1 + n_w:1 + n_w + n_extra]
    outs = refs[1 + n_w + n_extra:1 + n_w + n_extra + n_out]
    wb_scrs = refs[1 + n_w + n_extra + n_out:]

    @pl.when(pl.program_id(1) == 0)
    def _():
        for w_ref, wb_scr in zip(w_refs, wb_scrs):
            wb_scr[...] = w_ref[...].astype(wb_scr.dtype)

    x = x_ref[...]
    zs = [jnp.dot(x, wb_scr[...], preferred_element_type=F32) for wb_scr in wb_scrs]
    res = epilogue(*zs, *[e[...] for e in extras])
    for o_ref, r in zip(outs, res):
        o_ref[...] = r.astype(o_ref.dtype)


def _mm(x, ws, layer, col0, n_cols, epilogue, extras, out_dtypes, *, tm, tn, name):
    t, k = x.shape
    assert col0 % tn == 0 and n_cols % tn == 0 and t % tm == 0
    jb0 = col0 // tn
    in_specs = [pl.BlockSpec((tm, k), lambda j, i: (i, 0))]
    in_specs += [pl.BlockSpec((None, k, tn), lambda j, i: (layer, 0, jb0 + j)) for _ in ws]
    in_specs += [pl.BlockSpec((1, tn), lambda j, i: (0, j)) for _ in extras]
    return pl.pallas_call(
        functools.partial(_mm_kernel, epilogue=epilogue, n_w=len(ws), n_extra=len(extras), n_out=len(out_dtypes)),
        grid=(n_cols // tn, t // tm),
        in_specs=in_specs,
        out_specs=[pl.BlockSpec((tm, tn), lambda j, i: (i, j)) for _ in out_dtypes],
        out_shape=[jax.ShapeDtypeStruct((t, n_cols), dt) for dt in out_dtypes],
        scratch_shapes=[pltpu.VMEM((k, tn), BF16) for _ in ws],
        compiler_params=_params("arbitrary", "arbitrary"),
        name=name,
    )(x, *ws, *extras)


def _ep_silu(z):
    return (_silu(z),)


def _ep_identity(z):
    return (z,)


def _ep_forget(z, lb):
    e = jnp.exp(-jnp.abs(z))
    r = 1.0 / (1.0 + e)
    pos = z >= 0.0
    sig = jnp.where(pos, r, e * r)
    nsig = jnp.where(pos, e * r, r)
    one_m = 1.0 - lb
    log_f = jnp.log(jnp.maximum(lb, LB_FLOOR) + one_m * sig)
    return log_f, one_m * nsig


def _gelu_exact(z):
    return 0.5 * z * (1.0 + lax.erf(z * (2.0 ** -0.5)))


def _ep_gelu(z):
    return (_gelu_exact(z),)


def _ep_swiglu(gate, up):
    return (_silu(gate) * up,)


def _hgrn_rec_kernel(q_ref, k_ref, g_ref, v_ref, sg_ref, gain_ref, y_ref, b_scr, upd_scr, st_scr, *, hb, seq, cu):
    C, L, E = HG_CHUNK, HG_SUB, HG_EXPAND
    n_sub = C // L
    n_chunks = seq // C
    W = hb * E
    contract_last = (((1,), (1,)), ((), ()))
    contract_first = (((0,), (0,)), ((), ()))

    row = lax.broadcasted_iota(jnp.int32, (C, C), 0)
    col = lax.broadcasted_iota(jnp.int32, (C, C), 1)
    tril = (row >= col).astype(BF16)

    def pass1(ci, rmax):
        rows = [pl.ds(pl.multiple_of((ci * cu + u) * C, C), C) for u in range(cu)]
        gs, sums = [], []
        for u in range(cu):
            g = g_ref[rows[u], :]
            g1 = g.astype(BF16)
            r1 = g - g1.astype(F32)
            g2 = r1.astype(BF16)
            g3 = (r1 - g2.astype(F32)).astype(BF16)
            gs.append(g)
            sums.append(jnp.dot(tril, jnp.concatenate([g1, g2, g3], axis=1), preferred_element_type=F32))
        kdecs = []
        for u in range(cu):
            b = sums[u][:, :W] + sums[u][:, W:2 * W] + sums[u][:, 2 * W:]
            b_scr[rows[u], :] = b
            for j in range(n_sub):
                bstart = b[j * L:j * L + 1] - gs[u][j * L:j * L + 1]
                rmax = jnp.maximum(rmax, bstart - b[(j + 1) * L - 1:(j + 1) * L])
            kdecs.append((k_ref[rows[u], :].astype(F32) * jnp.exp(b[C - 1:C] - b)).astype(BF16))
        for u in range(cu):
            for h in range(hb):
                cols = slice(h * E, (h + 1) * E)
                upd_scr[ci * cu + u, h] = lax.dot_general(v_ref[rows[u], cols], kdecs[u][:, cols], contract_first,
                                                          preferred_element_type=F32)
        return rmax

    rmax = lax.fori_loop(0, n_chunks // cu, pass1, jnp.zeros((1, W), F32))
    safe = jnp.max(rmax) <= SAFE_DECAY_RANGE

    def scan(c, states):
        decay = jnp.exp(b_scr[pl.ds(c * C + C - 1, 1), :])
        new_states = []
        for h in range(hb):
            st_scr[c, h] = states[h].astype(BF16)
            new_states.append(states[h] * decay[:, h * E:(h + 1) * E] + upd_scr[c, h])
        return tuple(new_states)

    lax.fori_loop(0, n_chunks, scan, tuple(jnp.zeros((E, E), F32) for _ in range(hb)))

    def load(c, h):
        r0 = pl.multiple_of(c * C, C)
        rows, cols = pl.ds(r0, C), slice(h * E, (h + 1) * E)
        q = q_ref[rows, cols].astype(F32)
        k = k_ref[rows, cols].astype(F32)
        return rows, cols, q, k, v_ref[rows, cols], g_ref[rows, cols], b_scr[rows, cols]

    def finish(rows, cols, o):
        y = _rms(o, gain_ref[...]) * sg_ref[rows, cols].astype(F32)
        y_ref[rows, cols] = y.astype(y_ref.dtype)

    wr = lax.broadcasted_iota(jnp.int32, (C, n_sub * C), 0)
    wc = lax.broadcasted_iota(jnp.int32, (C, n_sub * C), 1)
    keep_wide = jnp.logical_and(wc >= (wr // L) * C, wc <= (wr // L) * C + wr)

    def fast_body(ci, carry):
        chunk_heads = [(ci * cu + u, h) for u in range(cu) for h in range(hb)]
        items = [load(c, h) for c, h in chunk_heads]
        inter, scores = [], []
        for (rows, cols, q, k, v, g, b), (c, h) in zip(items, chunk_heads):
            qb = (q * jnp.exp(b)).astype(BF16)
            inter.append(lax.dot_general(qb, st_scr[c, h], contract_last, preferred_element_type=F32))
            qf, kw = [], []
            for j in range(n_sub):
                lo, hi = j * L, (j + 1) * L
                bstart = b[lo:lo + 1] - g[lo:lo + 1]
                qf.append(q[lo:hi] * jnp.exp(b[lo:hi] - bstart))
                kw.append((k[:hi] * jnp.exp(bstart - b[:hi])).astype(BF16))
                if hi < C:
                    kw.append(jnp.zeros((C - hi, E), BF16))
            qf = jnp.concatenate(qf, axis=0).astype(BF16)
            kw = jnp.concatenate(kw, axis=0)
            scores.append(lax.dot_general(qf, kw, contract_last, preferred_element_type=F32))
        intra = []
        for (rows, cols, q, k, v, g, b), s in zip(items, scores):
            a = jnp.where(keep_wide, s, 0.0).astype(BF16)
            intra.append(jnp.dot(a, jnp.concatenate([v] * n_sub, axis=0), preferred_element_type=F32))
        for (rows, cols, q, k, v, g, b), o1, o2 in zip(items, inter, intra):
            finish(rows, cols, o1 + o2)
        return carry

    s_idx = lax.broadcasted_iota(jnp.int32, (L, 1), 0)

    def exact_body(c, carry):
        for h in range(hb):
            rows, cols, q, k, v, g, b = load(c, h)
            qb = (q * jnp.exp(b)).astype(BF16)
            o = lax.dot_general(qb, st_scr[c, h], contract_last, preferred_element_type=F32)
            pieces = []
            for j in range(n_sub):
                lo, hi = j * L, (j + 1) * L
                bstart = b[lo:lo + 1] - g[lo:lo + 1]
                qf = (q[lo:hi] * jnp.exp(b[lo:hi] - bstart)).astype(BF16)
                if j > 0:
                    kf = (k[:lo] * jnp.exp(bstart - b[:lo])).astype(BF16)
                    a = lax.dot_general(qf, kf, contract_last, preferred_element_type=F32)
                    oj = jnp.dot(a.astype(BF16), v[:lo], preferred_element_type=F32)
                else:
                    oj = jnp.zeros((L, E), F32)
                qj, kj, bj = q[lo:hi], k[lo:hi], b[lo:hi]
                vj = v[lo:hi].astype(F32)
                out_rows = []
                for t in range(L):
                    m = s_idx <= t
                    wdec = jnp.where(m, jnp.exp(jnp.where(m, bj[t:t + 1] - bj, 0.0)), 0.0)
                    a_col = jnp.sum(wdec * qj[t:t + 1] * kj, axis=-1, keepdims=True)
                    out_rows.append(jnp.sum(a_col * vj, axis=0, keepdims=True))
                pieces.append(oj + jnp.concatenate(out_rows, axis=0))
            finish(rows, cols, o + jnp.concatenate(pieces, axis=0))
        return carry

    @pl.when(safe)
    def _():
        lax.fori_loop(0, n_chunks // cu, fast_body, 0)

    @pl.when(jnp.logical_not(safe))
    def _():
        lax.fori_loop(0, n_chunks, exact_body, 0)


def _hgrn_recurrence(q, k, log_f, v, sg, gain, *, batch, seq, hb=2, cu=16):
    t, d = q.shape
    w = hb * HG_EXPAND
    n_chunks = seq // HG_CHUNK
    assert n_chunks % cu == 0
    spec = pl.BlockSpec((seq, w), lambda b, h: (b, h))
    return pl.pallas_call(
        functools.partial(_hgrn_rec_kernel, hb=hb, seq=seq, cu=cu),
        grid=(batch, d // w),
        in_specs=[spec, spec, spec, spec, spec, pl.BlockSpec((1, HG_EXPAND), lambda b, h: (0, 0))],
        out_specs=spec,
        out_shape=jax.ShapeDtypeStruct((t, d), BF16),
        scratch_shapes=[pltpu.VMEM((seq, w), F32),
                        pltpu.VMEM((n_chunks, hb, HG_EXPAND, HG_EXPAND), F32),
                        pltpu.VMEM((n_chunks, hb, HG_EXPAND, HG_EXPAND), BF16)],
        compiler_params=_params("parallel", "parallel"),
        name="hgrn_recurrence",
    )(q, k, log_f, v, sg, _row(gain))


def _rows(i):
    return (i, 0)


def _chunk(c, n_chunks, size):
    step = size // n_chunks
    return slice(c * step, (c + 1) * step)


def _pieces(ref, c, n_chunks):
    chunk = _chunk(c, n_chunks, ref.shape[0])
    rows = [slice(r0, r0 + ROW_PIECE) for r0 in range(chunk.start, chunk.stop, ROW_PIECE)]
    cols = [slice(c0, c0 + COL_PIECE) for c0 in range(0, ref.shape[1], COL_PIECE)]
    return rows, cols


def _residual_stage2(read, lag, outs, c, n_chunks):
    (m_ref,), (h_ref, gpost_ref, gnext_ref), (hout_ref, anext_ref) = read, lag, outs
    d = m_ref.shape[1]
    row_pieces, col_blocks = _pieces(m_ref, c, n_chunks)
    for rows in row_pieces:
        ss = sum(jnp.sum(jnp.square(m_ref[rows, cb]), axis=-1, keepdims=True) for cb in col_blocks)
        inv = lax.rsqrt(ss * (1.0 / d) + EPS)
        ss = 0.0
        for cb in col_blocks:
            hn = h_ref[rows, cb] + m_ref[rows, cb] * inv * gpost_ref[:, cb]
            hout_ref[rows, cb] = hn
            ss = ss + jnp.sum(jnp.square(hn), axis=-1, keepdims=True)
        if gnext_ref is None:
            for cb in col_blocks:
                anext_ref[rows, cb] = hout_ref[rows, cb].astype(anext_ref.dtype)
        else:
            inv = lax.rsqrt(ss * (1.0 / d) + EPS)
            for cb in col_blocks:
                anext_ref[rows, cb] = (hout_ref[rows, cb] * inv * gnext_ref[:, cb]).astype(anext_ref.dtype)


def _residual_plain_stage2(read, lag, outs, c, n_chunks):
    (h_ref, gpost_ref), (hout_ref, hb_ref) = lag, outs
    _residual_stage2(read, (h_ref, gpost_ref, None), (hout_ref, hb_ref), c, n_chunks)


def _dot_stage1(now, write, extra, c, n_chunks):
    (x_ref, w_ref), (m_ref,) = now, write
    cols = _chunk(c, n_chunks, w_ref.shape[1])
    m_ref[:, cols] = jnp.dot(x_ref[...], w_ref[:, cols], preferred_element_type=F32)


def _mm_residual(y, w, h, gpost, gnext, *, tm, name):
    t, k = y.shape
    d = w.shape[1]
    assert t % tm == 0
    lag = [_Blocked(h, (tm, d), _rows), _Blocked(_row(gpost), (1, d), _const2)]
    if gnext is not None:
        lag.append(_Blocked(_row(gnext), (1, d), _const2))
    return _lagged_call(
        _dot_stage1, _residual_plain_stage2 if gnext is None else _residual_stage2,
        now=[_Blocked(y, (tm, k), _rows), _Blocked(w, (k, d), _const2, resident=True)],
        lag=lag,
        outs=[_Blocked(jax.ShapeDtypeStruct((t, d), F32), (tm, d), _rows),
              _Blocked(jax.ShapeDtypeStruct((t, d), BF16), (tm, d), _rows)],
        stash=[(tm, d)], n_tiles=t // tm, n_chunks=8, name=name)


def _sgu_stage1(now, write, extra, c, n_chunks):
    (u_ref, v_ref, ws_ref, bst_ref, w_ref), (m_ref,), (y_scr,) = now, write, extra
    if c == 0:
        P, G = GM_CHUNK, GM_GROUPS
        gd = u_ref.shape[1] // G
        causal = lax.broadcasted_iota(jnp.int32, (P, P), 0) >= lax.broadcasted_iota(jnp.int32, (P, P), 1)

        def chunk(ci, carry):
            r0 = pl.multiple_of(ci * P, P)
            for g in range(G):
                cols = slice(g * gd, (g + 1) * gd)
                wg = jnp.where(causal, ws_ref[g], jnp.zeros((), ws_ref.dtype))
                sv = jnp.dot(wg, v_ref[pl.ds(r0, P), cols], preferred_element_type=F32)
                sv = sv + bst_ref[:, g:g + 1]
                y = u_ref[pl.ds(r0, P), cols].astype(F32) * sv
                y_scr[pl.ds(r0, P), cols] = y.astype(y_scr.dtype)
            return carry

        lax.fori_loop(0, u_ref.shape[0] // P, chunk, 0)
    cols = _chunk(c, n_chunks, w_ref.shape[1])
    m_ref[:, cols] = jnp.dot(y_scr[...], w_ref[:, cols], preferred_element_type=F32)


def _sgu_out(u, v, ws, bs_t, w, h, gpost, gnext, *, tm):
    t, d = u.shape
    return _lagged_call(
        _sgu_stage1, _residual_stage2,
        now=[_Blocked(u, (tm, d), _rows), _Blocked(v, (tm, d), _rows),
             _Blocked(ws, ws.shape, lambda i: (0, 0, 0)), _Blocked(bs_t, bs_t.shape, _const2),
             _Blocked(w, w.shape, _const2, resident=True)],
        lag=[_Blocked(h, (tm, d), _rows), _Blocked(_row(gpost), (1, d), _const2), _Blocked(_row(gnext), (1, d), _const2)],
        outs=[_Blocked(jax.ShapeDtypeStruct((t, d), F32), (tm, d), _rows),
              _Blocked(jax.ShapeDtypeStruct((t, d), BF16), (tm, d), _rows)],
        stash=[(tm, d)], n_tiles=t // tm, n_chunks=8,
        extra_scratch=[pltpu.VMEM((tm, d), BF16)], name="gmlp_sgu_out")


def _gelu_layernorm_stage2(read, lag, outs, c, n_chunks):
    (z_ref,), (g_ref, b_ref), (o_ref,) = read, lag, outs
    d = z_ref.shape[1]
    row_pieces, col_blocks = _pieces(z_ref, c, n_chunks)
    for rows in row_pieces:
        s1 = 0.0
        for cb in col_blocks:
            y = _gelu_exact(z_ref[rows, cb])
            z_ref[rows, cb] = y
            s1 = s1 + jnp.sum(y, axis=-1, keepdims=True)
        mu = s1 * (1.0 / d)
        s2 = sum(jnp.sum(jnp.square(z_ref[rows, cb] - mu), axis=-1, keepdims=True) for cb in col_blocks)
        inv = lax.rsqrt(s2 * (1.0 / d) + EPS)
        for cb in col_blocks:
            o_ref[rows, cb] = ((z_ref[rows, cb] - mu) * inv * g_ref[:, cb] + b_ref[:, cb]).astype(o_ref.dtype)


def _mm_gelu_layernorm(x, w, col_block, g, b, *, tm, name):
    t, k = x.shape
    d = g.shape[0]
    return _lagged_call(
        _dot_stage1, _gelu_layernorm_stage2,
        now=[_Blocked(x, (tm, k), _rows), _Blocked(w, (k, d), lambda i: (0, col_block), resident=True)],
        lag=[_Blocked(_row(g), (1, d), _const2), _Blocked(_row(b), (1, d), _const2)],
        outs=[_Blocked(jax.ShapeDtypeStruct((t, d), BF16), (tm, d), _rows)],
        stash=[(tm, d)], n_tiles=t // tm, n_chunks=8, name=name)


def _ple_stage1(now, write, extra, c, n_chunks):
    (p_ref, hb_ref, wp_ref, wg_ref), (z_ref,) = now, write
    cols = _chunk(c, n_chunks, wg_ref.shape[1])
    e = jnp.dot(p_ref[...].astype(BF16), wp_ref[:, cols], preferred_element_type=F32)
    gate = _sigmoid(jnp.dot(hb_ref[...], wg_ref[:, cols], preferred_element_type=F32))
    z_ref[:, cols] = e * gate


def _ple(p, h, hb, wp, wg, gple, gnext, *, tm):
    t, d = h.shape
    pd = p.shape[1]
    return _lagged_call(
        _ple_stage1, _residual_stage2,
        now=[_Blocked(p, (tm, pd), _rows), _Blocked(hb, (tm, d), _rows),
             _Blocked(wp, wp.shape, _const2, resident=True), _Blocked(wg, wg.shape, _const2, resident=True)],
        lag=[_Blocked(h, (tm, d), _rows), _Blocked(_row(gple), (1, d), _const2), _Blocked(_row(gnext), (1, d), _const2)],
        outs=[_Blocked(jax.ShapeDtypeStruct((t, d), F32), (tm, d), _rows),
              _Blocked(jax.ShapeDtypeStruct((t, d), BF16), (tm, d), _rows)],
        stash=[(tm, d)], n_tiles=t // tm, n_chunks=8, name="ple_gate")


def kernel(x, p, hg_w_in, hg_lb_logits, hg_out_norm, hg_w_out, gm_w_in, gm_ln_g, gm_ln_b, gm_w_s, gm_b_s,
           gm_w_out, norm_mix_pre, norm_mix_post, norm_ffn_pre, norm_ffn_post, ffn_w_gate, ffn_w_up,
           ffn_w_down, ple_w_proj, ple_w_gate, ple_norm):
    batch, seq, d = x.shape
    depth = p.shape[0]
    t = batch * seq
    fd = hg_lb_logits.shape[1]
    gw = gm_w_in.shape[2] // 2
    assert seq % HG_CHUNK == 0 and seq % GM_CHUNK == 0 and fd == d and gw == d

    lbs = _lower_bounds(hg_lb_logits)
    h = x.reshape(t, d)
    a = _prenorm(h, norm_mix_pre[0])
    ones = jnp.ones((d,), F32)

    for i in range(depth):
        j = i // 2
        if i % 2 == 0:
            (q,) = _mm(a, [hg_w_in], j, 0, fd, _ep_silu, [], [BF16], tm=1024, tn=1024, name="hgrn_in_q")
            log_f, kk = _mm(a, [hg_w_in], j, fd, fd, _ep_forget, [lbs[j].reshape(1, fd)], [F32, BF16],
                            tm=1024, tn=1024, name="hgrn_in_f")
            (v,) = _mm(a, [hg_w_in], j, 2 * fd, d, _ep_identity, [], [BF16], tm=1024, tn=1024, name="hgrn_in_v")
            (sg,) = _mm(a, [hg_w_in], j, 2 * fd + d, d, _ep_silu, [], [BF16], tm=1024, tn=1024, name="hgrn_in_g")
            y = _hgrn_recurrence(q, kk, log_f, v, sg, hg_out_norm[j], batch=batch, seq=seq)
            h, a = _mm_residual(y, hg_w_out[j].astype(BF16), h, norm_mix_post[i], norm_ffn_pre[i],
                                tm=512, name="hgrn_out")
        else:
            (u,) = _mm(a, [gm_w_in], j, 0, gw, _ep_gelu, [], [BF16], tm=1024, tn=1024, name="gmlp_in_u")
            (v,) = _mm_gelu_layernorm(a, gm_w_in[j, :, gw:].astype(BF16), 0, gm_ln_g[j], gm_ln_b[j], tm=512,
                                      name="gmlp_in_v")
            h, a = _sgu_out(u, v, gm_w_s[j].astype(BF16), gm_b_s[j].T.astype(F32), gm_w_out[j].astype(BF16), h,
                            norm_mix_post[i], norm_ffn_pre[i], tm=512)

        (act,) = _mm(a, [ffn_w_gate, ffn_w_up], i, 0, ffn_w_gate.shape[2], _ep_swiglu, [], [BF16],
                     tm=1024, tn=512, name="ffn_in")
        h, hb = _mm_residual(act, ffn_w_down[i].astype(BF16), h, norm_ffn_post[i], None, tm=256, name="ffn_out")
        gnext = norm_mix_pre[i + 1] if i + 1 < depth else ones
        h, a = _ple(p[i].reshape(t, -1), h, hb, ple_w_proj[i].astype(BF16), ple_w_gate[i].astype(BF16),
                    ple_norm[i], gnext, tm=512)

    return h.reshape(batch, seq, d)
```
